```python
import jax, jax.numpy as jnp
from jax import lax
import numpy as np

D_MODEL = 1024
BATCH = 16
SEQ = 4096
DEPTH = 1
DEC_BATCH = 128
DEC_SEQ = 1
PAST_LEN = 8192
PAGE_SIZE = 128

N_HEADS = 8
HEAD_DIM = 64
ATTN_WIDTH = N_HEADS * HEAD_DIM
CONV_WIDTH = D_MODEL - ATTN_WIDTH
CONV_K = 31
IN_COLS = 3 * ATTN_WIDTH + N_HEADS + 2 * CONV_WIDTH
N_GROUPS = 4
EXPERTS_PER_GROUP = 4
N_EXPERTS = N_GROUPS * EXPERTS_PER_GROUP
TOP_K = 2
D_EXPERT = D_MODEL // 2
Q_BLOCK = 128
N_MOD = 6
EPS = 1e-6
FORGET_BIAS_INIT = 3.0

kernel_name = 'hymba_fox_conformer_hmoe_step'

F32 = jnp.float32


def rmsnorm(x, g):
    xf = x.astype(F32)
    y = xf * lax.rsqrt(jnp.mean(xf * xf, axis=-1, keepdims=True) + EPS)
    return (y * g.astype(F32)).astype(x.dtype)


def ada_modulation(c, w, b, n):
    m = jax.nn.silu(c) @ w + b
    return m.reshape(c.shape[0], n, D_MODEL)


def modulate(h, shift, scale):
    return h * (1 + scale[:, None, :]) + shift[:, None, :]


def split_projection(z, b_f):
    A = ATTN_WIDTH
    lead = z.shape[:-1]
    q = z[..., :A].reshape(*lead, N_HEADS, HEAD_DIM)
    k = z[..., A:2 * A].reshape(*lead, N_HEADS, HEAD_DIM)
    v = z[..., 2 * A:3 * A].reshape(*lead, N_HEADS, HEAD_DIM)
    o = 3 * A
    logf = jax.nn.log_sigmoid(z[..., o:o + N_HEADS].astype(F32) + b_f.astype(F32))
    o = o + N_HEADS
    u = z[..., o:o + CONV_WIDTH] * jax.nn.sigmoid(z[..., o + CONV_WIDTH:o + 2 * CONV_WIDTH])
    return q, k, v, logf, u


def fox_attention_prompt(q, k, v, logf):
    B, S = q.shape[0], q.shape[1]
    nb = S // Q_BLOCK
    scale = HEAD_DIM ** -0.5
    F = jnp.cumsum(logf, axis=1)
    Fk = F.transpose(0, 2, 1)[:, :, None, :]
    qb = q.reshape(B, nb, Q_BLOCK, N_HEADS, HEAD_DIM).swapaxes(0, 1)
    Fqb = F.reshape(B, nb, Q_BLOCK, N_HEADS).swapaxes(0, 1)
    kpos = jnp.arange(S)

    def block(args):
        qi, Fqi, i = args
        s = jnp.einsum('bqhd,bkhd->bhqk', qi, k, preferred_element_type=F32) * scale
        s = s + Fqi.transpose(0, 2, 1)[..., None] - Fk
        qpos = i * Q_BLOCK + jnp.arange(Q_BLOCK)
        s = jnp.where(kpos[None, :] <= qpos[:, None], s, -jnp.inf)
        p = jax.nn.softmax(s, axis=-1)
        return jnp.einsum('bhqk,bkhd->bqhd', p.astype(v.dtype), v)

    o = lax.map(block, (qb, Fqb, jnp.arange(nb)))
    return o.swapaxes(0, 1).reshape(B, S, ATTN_WIDTH)


def fox_attention_sample(q, k, v, logf, cache_k, cache_v, cache_logf, page_table):
    Bd, T = q.shape[0], q.shape[1]
    n_pages = page_table.shape[1]
    scale = HEAD_DIM ** -0.5
    q32 = q.astype(F32)
    logf_past = cache_logf[page_table].astype(F32).reshape(Bd, n_pages * PAGE_SIZE, N_HEADS)
    F_past = jnp.cumsum(logf_past, axis=1)
    F_new = F_past[:, -1:, :] + jnp.cumsum(logf, axis=1)
    Fq = F_new.transpose(0, 2, 1)[..., None]
    s = jnp.einsum('bqhd,bkhd->bhqk', q32, k.astype(F32)) * scale
    s = s + Fq - F_new.transpose(0, 2, 1)[:, :, None, :]
    s = jnp.where(jnp.tril(jnp.ones((T, T), dtype=bool)), s, -jnp.inf)
    m = jnp.max(s, axis=-1)
    p = jnp.exp(s - m[..., None])
    l = jnp.sum(p, axis=-1)
    acc = jnp.einsum('bhqk,bkhd->bhqd', p, v.astype(F32))
    Fk_pages = F_past.reshape(Bd, n_pages, PAGE_SIZE, N_HEADS).transpose(1, 0, 3, 2)

    def step(carry, xs):
        m, l, acc = carry
        pid, Fk = xs
        kp = cache_k[pid].astype(F32)
        vp = cache_v[pid].astype(F32)
        s = jnp.einsum('bqhd,bkhd->bhqk', q32, kp) * scale + Fq - Fk[:, :, None, :]
        m_new = jnp.maximum(m, jnp.max(s, axis=-1))
        alpha = jnp.exp(m - m_new)
        p = jnp.exp(s - m_new[..., None])
        l = l * alpha + jnp.sum(p, axis=-1)
        acc = acc * alpha[..., None] + jnp.einsum('bhqk,bkhd->bhqd', p, vp)
        return (m_new, l, acc), None

    (m, l, acc), _ = lax.scan(step, (m, l, acc), (page_table.T, Fk_pages))
    o = acc / l[..., None]
    return o.transpose(0, 2, 1, 3).reshape(Bd, T, ATTN_WIDTH).astype(q.dtype)


def depthwise_valid(u, conv_w):
    return lax.conv_general_dilated(u, conv_w.astype(u.dtype)[:, None, :], window_strides=(1,),
                                    padding='VALID', dimension_numbers=('NWC', 'WIO', 'NWC'),
                                    feature_group_count=u.shape[-1])


def conv_tail(y, conv_b, ln_g, ln_b):
    yf = (y + conv_b).astype(F32)
    mu = jnp.mean(yf, axis=-1, keepdims=True)
    d = yf - mu
    yn = d * lax.rsqrt(jnp.mean(d * d, axis=-1, keepdims=True) + EPS) * ln_g.astype(F32) + ln_b.astype(F32)
    return (yn * jax.nn.sigmoid(yn)).astype(y.dtype)


def conv_module_prompt(u, conv_w, conv_b, ln_g, ln_b):
    upad = jnp.pad(u, ((0, 0), (CONV_K - 1, 0), (0, 0)))
    return conv_tail(depthwise_valid(upad, conv_w), conv_b, ln_g, ln_b), u[:, -(CONV_K - 1):]


def conv_module_sample(u, state, conv_w, conv_b, ln_g, ln_b):
    ucat = jnp.concatenate([state.astype(u.dtype), u], axis=1)
    return conv_tail(depthwise_valid(ucat, conv_w), conv_b, ln_g, ln_b), ucat[:, -(CONV_K - 1):]


def hier_moe(h, w_rg, b_rg, w_re, b_re, w1, w3, w2):
    shp = h.shape
    t = h.reshape(-1, D_MODEL)
    g_logits = (t @ w_rg).astype(F32) + b_rg.astype(F32)
    g_prob = jax.nn.softmax(g_logits, axis=-1)
    g_idx = jnp.argmax(g_logits, axis=-1)
    g_w = jnp.take_along_axis(g_prob, g_idx[:, None], axis=-1)
    e_logits = ((t @ w_re).astype(F32) + b_re.astype(F32)).reshape(-1, N_GROUPS, EXPERTS_PER_GROUP)
    e_logits = jnp.take_along_axis(e_logits, g_idx[:, None, None], axis=1)[:, 0]
    top_v, top_i = lax.top_k(e_logits, TOP_K)
    e_w = jax.nn.softmax(top_v, axis=-1) * g_w
    eid = g_idx[:, None] * EXPERTS_PER_GROUP + top_i
    gate = jnp.sum(jax.nn.one_hot(eid, N_EXPERTS, dtype=F32) * e_w[..., None], axis=1)
    y = jnp.zeros(t.shape, F32)
    for e in range(N_EXPERTS):
        he = jax.nn.silu(t @ w1[e]) * (t @ w3[e])
        y = y + gate[:, e:e + 1] * (he @ w2[e]).astype(F32)
    return y.astype(h.dtype).reshape(shp)


def setup_inputs(seed: int = 0) -> dict:
    key = jax.random.key(seed)
    ks = jax.random.split(key, 40)
    n_pages = PAST_LEN // PAGE_SIZE
    n_used = DEC_BATCH * n_pages
    n_phys = n_used + n_used // 4

    def nrm(k, shape, s):
        return jax.random.normal(k, shape, F32) * s

    d_s = D_MODEL ** -0.5
    return {
        'x_prompt': nrm(ks[0], (BATCH, SEQ, D_MODEL), 1.0),
        'x_sample': nrm(ks[1], (DEC_BATCH, DEC_SEQ, D_MODEL), 1.0),
        'cache_k': nrm(ks[2], (DEPTH, n_phys, PAGE_SIZE, N_HEADS, HEAD_DIM), 1.0),
        'cache_v': nrm(ks[3], (DEPTH, n_phys, PAGE_SIZE, N_HEADS, HEAD_DIM), 1.0),
        'cache_logf': jax.nn.log_sigmoid(FORGET_BIAS_INIT + nrm(ks[4], (DEPTH, n_phys, PAGE_SIZE, N_HEADS), 1.0)),
        'state_conv': nrm(ks[5], (DEPTH, DEC_BATCH, CONV_K - 1, CONV_WIDTH), 0.5),
        'page_table': jax.random.permutation(ks[6], n_phys)[:n_used].reshape(DEC_BATCH, n_pages).astype(jnp.int32),
        'c_prompt': nrm(ks[7], (BATCH, D_MODEL), 1.0),
        'c_sample': nrm(ks[8], (DEC_BATCH, D_MODEL), 1.0),
        'w_ada': nrm(ks[9], (DEPTH, D_MODEL, N_MOD * D_MODEL), 0.5 * d_s),
        'b_ada': nrm(ks[10], (DEPTH, N_MOD * D_MODEL), 0.02),
        'g_norm1': 1.0 + nrm(ks[11], (DEPTH, D_MODEL), 0.02),
        'g_norm2': 1.0 + nrm(ks[12], (DEPTH, D_MODEL), 0.02),
        'w_in': nrm(ks[13], (DEPTH, D_MODEL, IN_COLS), d_s),
        'b_f': FORGET_BIAS_INIT + nrm(ks[14], (DEPTH, N_HEADS), 0.5),
        'conv_w': nrm(ks[15], (DEPTH, CONV_K, CONV_WIDTH), CONV_K ** -0.5),
        'conv_b': nrm(ks[16], (DEPTH, CONV_WIDTH), 0.02),
        'conv_ln_g': 1.0 + nrm(ks[17], (DEPTH, CONV_WIDTH), 0.02),
        'conv_ln_b': nrm(ks[18], (DEPTH, CONV_WIDTH), 0.02),
        'w_out': nrm(ks[19], (DEPTH, D_MODEL, D_MODEL), d_s),
        'w_rg': nrm(ks[20], (DEPTH, D_MODEL, N_GROUPS), d_s),
        'b_rg': nrm(ks[21], (DEPTH, N_GROUPS), 0.01),
        'w_re': nrm(ks[22], (DEPTH, D_MODEL, N_EXPERTS), d_s),
        'b_re': nrm(ks[23], (DEPTH, N_EXPERTS), 0.01),
        'w1': nrm(ks[24], (DEPTH, N_EXPERTS, D_MODEL, D_EXPERT), d_s),
        'w3': nrm(ks[25], (DEPTH, N_EXPERTS, D_MODEL, D_EXPERT), d_s),
        'w2': nrm(ks[26], (DEPTH, N_EXPERTS, D_EXPERT, D_MODEL), D_EXPERT ** -0.5),
        'w_ada_f': nrm(ks[27], (D_MODEL, 2 * D_MODEL), 0.5 * d_s),
        'b_ada_f': nrm(ks[28], (2 * D_MODEL,), 0.02),
        'g_final': 1.0 + nrm(ks[29], (D_MODEL,), 0.02),
    }


def reference(x_prompt, x_sample, cache_k, cache_v, cache_logf, state_conv, page_table, c_prompt, c_sample,
              w_ada, b_ada, g_norm1, g_norm2, w_in, b_f, conv_w, conv_b, conv_ln_g, conv_ln_b, w_out,
              w_rg, b_rg, w_re, b_re, w1, w3, w2, w_ada_f, b_ada_f, g_final):
    xp, xs = x_prompt, x_sample
    kp_l, vp_l, fp_l, cp_l = [], [], [], []
    ks_l, vs_l, fs_l, cs_l = [], [], [], []
    for i in range(DEPTH):
        mp = ada_modulation(c_prompt, w_ada[i], b_ada[i], N_MOD)
        ms = ada_modulation(c_sample, w_ada[i], b_ada[i], N_MOD)

        hp = modulate(rmsnorm(xp, g_norm1[i]), mp[:, 0], mp[:, 1])
        q, k, v, logf, u = split_projection(hp @ w_in[i], b_f[i])
        a = fox_attention_prompt(q, k, v, logf)
        cv, st = conv_module_prompt(u, conv_w[i], conv_b[i], conv_ln_g[i], conv_ln_b[i])
        xp = xp + mp[:, 2, None] * (jnp.concatenate([a, cv], axis=-1) @ w_out[i])
        kp_l.append(k); vp_l.append(v); fp_l.append(logf); cp_l.append(st)

        hs = modulate(rmsnorm(xs, g_norm1[i]), ms[:, 0], ms[:, 1])
        q, k, v, logf, u = split_projection(hs @ w_in[i], b_f[i])
        a = fox_attention_sample(q, k, v, logf, cache_k[i], cache_v[i], cache_logf[i], page_table)
        cv, st = conv_module_sample(u, state_conv[i], conv_w[i], conv_b[i], conv_ln_g[i], conv_ln_b[i])
        xs = xs + ms[:, 2, None] * (jnp.concatenate([a, cv], axis=-1) @ w_out[i])
        ks_l.append(k); vs_l.append(v); fs_l.append(logf); cs_l.append(st)

        xp = xp + mp[:, 5, None] * hier_moe(modulate(rmsnorm(xp, g_norm2[i]), mp[:, 3], mp[:, 4]),
                                            w_rg[i], b_rg[i], w_re[i], b_re[i], w1[i], w3[i], w2[i])
        xs = xs + ms[:, 5, None] * hier_moe(modulate(rmsnorm(xs, g_norm2[i]), ms[:, 3], ms[:, 4]),
                                            w_rg[i], b_rg[i], w_re[i], b_re[i], w1[i], w3[i], w2[i])

    fp_mod = ada_modulation(c_prompt, w_ada_f, b_ada_f, 2)
    fs_mod = ada_modulation(c_sample, w_ada_f, b_ada_f, 2)
    y_prompt = modulate(rmsnorm(xp, g_final), fp_mod[:, 0], fp_mod[:, 1])
    y_sample = modulate(rmsnorm(xs, g_final), fs_mod[:, 0], fs_mod[:, 1])
    return (y_prompt, y_sample,
            jnp.stack(kp_l), jnp.stack(vp_l), jnp.stack(fp_l), jnp.stack(cp_l),
            jnp.stack(ks_l), jnp.stack(vs_l), jnp.stack(fs_l), jnp.stack(cs_l))
```

```python
import functools

import jax
import jax.numpy as jnp
from jax import lax
from jax.experimental import pallas as pl
from jax.experimental.pallas import tpu as pltpu

F32 = jnp.float32
BF16 = jnp.bfloat16

EPS = 1e-6
N_HEADS = 8
HEAD_DIM = 64
ATTN_WIDTH = N_HEADS * HEAD_DIM
CONV_K = 31
N_GROUPS = 4
EXPERTS_PER_GROUP = 4
N_EXPERTS = N_GROUPS * EXPERTS_PER_GROUP
N_MOD = 6
LANES = 128
AUG_PER_HEAD = 6
LOGF_LANE0 = AUG_PER_HEAD * N_HEADS
CONV_HALO = 32
NEG_BIG = -1e30
VMEM_LIMIT = 56 * 1024 * 1024


def _dot(a, b):
    return jnp.dot(a, b, preferred_element_type=F32)


def _dot_nt(a, b):
    return lax.dot_general(a, b, (((1,), (1,)), ((), ())), preferred_element_type=F32)


def _split2(x):
    hi = x.astype(BF16)
    lo = (x - hi.astype(F32)).astype(BF16)
    return hi, lo


def _split3(x):
    hi = x.astype(BF16)
    r1 = x - hi.astype(F32)
    mid = r1.astype(BF16)
    lo = (r1 - mid.astype(F32)).astype(BF16)
    return hi, mid, lo


def _dot_hp(a, b):
    ah, al = _split2(a)
    bh, bl = _split2(b)
    return _dot(ah, bh) + (_dot(ah, bl) + _dot(al, bh))


def _sigmoid(x):
    return 1.0 / (1.0 + jnp.exp(-x))


def _log_sigmoid(x):
    return jnp.minimum(x, 0.0) - jnp.log1p(jnp.exp(-jnp.abs(x)))


def _rms(x, g):
    return x * lax.rsqrt(jnp.mean(x * x, axis=-1, keepdims=True) + EPS) * g


def _cparams(sem):
    return pltpu.CompilerParams(dimension_semantics=sem, vmem_limit_bytes=VMEM_LIMIT)


def _ada_kernel(c_ref, w_ref, b_ref, o_ref):
    c = c_ref[...]
    o_ref[...] = _dot_hp(c * _sigmoid(c), w_ref[...]) + b_ref[...]


def _ada(c, w, b, bn=1024):
    r, d = c.shape
    n = w.shape[1]
    return pl.pallas_call(
        _ada_kernel,
        grid=(n // bn,),
        in_specs=[pl.BlockSpec((r, d), lambda j: (0, 0)),
                  pl.BlockSpec((d, bn), lambda j: (0, j)),
                  pl.BlockSpec((1, bn), lambda j: (0, j))],
        out_specs=pl.BlockSpec((r, bn), lambda j: (0, j)),
        out_shape=jax.ShapeDtypeStruct((r, n), F32),
        compiler_params=_cparams(("arbitrary",)),
        name="ada",
    )(c, w, b.reshape(1, n))


def _conv_tail(y, conv_b, ln_g, ln_b):
    yf = y + conv_b
    mu = jnp.mean(yf, axis=-1, keepdims=True)
    d = yf - mu
    yn = d * lax.rsqrt(jnp.mean(d * d, axis=-1, keepdims=True) + EPS) * ln_g + ln_b
    return yn * _sigmoid(yn)


def _inproj_kernel(*refs, tm, n_i, prompt, conv_rows):
    if prompt:
        (x_ref, mod_ref, g_ref, w_ref, bf_ref, cw_ref, cb_ref, lg_ref, lb_ref,
         q_ref, k_ref, v_ref, lf_ref, kb_ref, vb_ref, qa_ref, ka_ref, cv_ref, st_ref,
         ubuf, fcarry) = refs
    else:
        (x_ref, mod_ref, g_ref, w_ref, bf_ref,
         q_ref, k_ref, v_ref, lf_ref, u_ref) = refs
    i = pl.program_id(1)
    a = ATTN_WIDTH
    x = x_ref[0]
    m = mod_ref[0]
    hb = (_rms(x, g_ref[...]) * (1.0 + m[1]) + m[0]).astype(BF16)

    q_ref[0] = (_dot(hb, w_ref[:, 0:a]) * (HEAD_DIM ** -0.5)).astype(BF16)
    k = _dot(hb, w_ref[:, a:2 * a])
    k_ref[0] = k
    v = _dot(hb, w_ref[:, 2 * a:3 * a])
    v_ref[0] = v
    cwid = (w_ref.shape[1] - 3 * a - LANES) // 2
    o = 3 * a
    u = _dot(hb, w_ref[:, o:o + cwid]) * _sigmoid(_dot(hb, w_ref[:, o + cwid:o + 2 * cwid]))
    lf = _log_sigmoid(_dot(hb, w_ref[:, o + 2 * cwid:]) + bf_ref[...])
    lf_ref[0] = lf[:, LOGF_LANE0:LOGF_LANE0 + N_HEADS]
    if not prompt:
        u_ref[0] = u
        return

    kb_ref[0] = k.astype(BF16)
    vb_ref[0] = v.astype(BF16)

    @pl.when(i == 0)
    def _():
        ubuf[0:CONV_HALO, :] = jnp.zeros((CONV_HALO, cwid), F32)
        fcarry[...] = jnp.zeros_like(fcarry)

    cb = 256 if tm % 256 == 0 else tm
    row = lax.broadcasted_iota(jnp.int32, (cb, cb), 0)
    col = lax.broadcasted_iota(jnp.int32, (cb, cb), 1)
    tri = (row >= col).astype(BF16)
    parts = jnp.concatenate(_split3(lf), axis=1)
    carry = fcarry[...]
    lane = lax.broadcasted_iota(jnp.int32, (1, LANES), 1)
    r = lane % AUG_PER_HEAD
    valid = lane < LOGF_LANE0
    for c in range(tm // cb):
        p3 = _dot(tri, parts[c * cb:(c + 1) * cb, :])
        fc = (p3[:, 0:LANES] + p3[:, LANES:2 * LANES] + p3[:, 2 * LANES:]) + carry
        carry = fc[cb - 1:cb, :]
        fh = fc.astype(BF16).astype(F32)
        r1 = fc - fh
        fm = r1.astype(BF16).astype(F32)
        fl = (r1 - fm).astype(BF16).astype(F32)
        qa = jnp.where(r == 0, fh, jnp.where(r == 1, fm, jnp.where(r == 2, fl, 1.0)))
        ka = jnp.where(r < 3, 1.0, jnp.where(r == 3, -fh, jnp.where(r == 4, -fm, -fl)))
        qa_ref[0, c * cb:(c + 1) * cb, :] = jnp.where(valid, qa, 0.0).astype(BF16)
        ka_ref[0, c * cb:(c + 1) * cb, :] = jnp.where(valid, ka, 0.0).astype(BF16)
    fcarry[...] = carry

    ubuf[CONV_HALO:CONV_HALO + tm, :] = u
    off = CONV_HALO - (CONV_K - 1)
    for c in range(tm // conv_rows):
        base = c * conv_rows + off
        acc = cw_ref[0:1, :] * ubuf[base:base + conv_rows, :]
        for j in range(1, CONV_K):
            acc = acc + cw_ref[j:j + 1, :] * ubuf[base + j:base + j + conv_rows, :]
        cv_ref[0, c * conv_rows:(c + 1) * conv_rows, :] = _conv_tail(
            acc, cb_ref[...], lg_ref[...], lb_ref[...]).astype(BF16)

    @pl.when(i == n_i - 1)
    def _():
        st_ref[0] = ubuf[CONV_HALO + tm - (CONV_K - 1):CONV_HALO + tm, :]

    ubuf[0:CONV_HALO, :] = ubuf[tm:tm + CONV_HALO, :]


def _inproj(x, mod, g, w_cat, bf_pad, conv=None, tm=512):
    b, s, d = x.shape
    tm = min(tm, s)
    n_i = s // tm
    a = ATTN_WIDTH
    cwid = (w_cat.shape[1] - 3 * a - LANES) // 2
    rmod = mod.shape[2]
    mod_spec = (pl.BlockSpec((1, N_MOD, 1, d), lambda bi, i: (bi, 0, 0, 0)) if rmod == 1 else
                pl.BlockSpec((1, N_MOD, tm, d), lambda bi, i: (bi, 0, i, 0)))
    tile = lambda w: pl.BlockSpec((1, tm, w), lambda bi, i: (bi, i, 0))
    full = lambda arr: pl.BlockSpec(arr.shape, lambda bi, i: (0,) * arr.ndim)
    prompt = conv is not None
    ins = [x, mod, g, w_cat, bf_pad]
    in_specs = [tile(d), mod_spec, full(g), full(w_cat), full(bf_pad)]
    outs = [jax.ShapeDtypeStruct((b, s, a), BF16), jax.ShapeDtypeStruct((b, s, a), F32),
            jax.ShapeDtypeStruct((b, s, a), F32), jax.ShapeDtypeStruct((b, s, N_HEADS), F32)]
    out_specs = [tile(a), tile(a), tile(a), tile(N_HEADS)]
    scratch = []
    if prompt:
        ins += list(conv)
        in_specs += [full(c) for c in conv]
        outs += [jax.ShapeDtypeStruct((b, s, a), BF16), jax.ShapeDtypeStruct((b, s, a), BF16),
                 jax.ShapeDtypeStruct((b, s, LANES), BF16), jax.ShapeDtypeStruct((b, s, LANES), BF16),
                 jax.ShapeDtypeStruct((b, s, cwid), BF16), jax.ShapeDtypeStruct((b, CONV_K - 1, cwid), F32)]
        out_specs += [tile(a), tile(a), tile(LANES), tile(LANES), tile(cwid),
                      pl.BlockSpec((1, CONV_K - 1, cwid), lambda bi, i: (bi, 0, 0))]
        scratch = [pltpu.VMEM((tm + CONV_HALO, cwid), F32), pltpu.VMEM((1, LANES), F32)]
    else:
        outs += [jax.ShapeDtypeStruct((b, s, cwid), F32)]
        out_specs += [tile(cwid)]
    return pl.pallas_call(
        functools.partial(_inproj_kernel, tm=tm, n_i=n_i, prompt=prompt, conv_rows=min(32, tm)),
        grid=(b, n_i),
        in_specs=in_specs,
        out_specs=out_specs,
        out_shape=outs,
        scratch_shapes=scratch,
        compiler_params=_cparams(("arbitrary", "arbitrary")),
        name="inproj_prompt" if prompt else "inproj_sample",
    )(*ins)


def _attn_kernel(q_ref, qa_ref, k_ref, ka_ref, v_ref, o_ref, *, tq):
    i = pl.program_id(1)
    lane = lax.broadcasted_iota(jnp.int32, (1, LANES), 1)
    lo = lane < HEAD_DIM
    qa = qa_ref[0]
    rowi = lax.broadcasted_iota(jnp.int32, (tq, tq), 0)
    coli = lax.broadcasted_iota(jnp.int32, (tq, tq), 1)
    causal = coli <= rowi
    zero = jnp.zeros((), BF16)

    for j in range(N_HEADS // 2):
        sl = slice(j * LANES, (j + 1) * LANES)
        q2 = q_ref[0, :, sl]
        lhs = []
        for hh in range(2):
            h = 2 * j + hh
            qm = jnp.where(lo if hh == 0 else jnp.logical_not(lo), q2, zero)
            am = jnp.where((lane >= AUG_PER_HEAD * h) & (lane < AUG_PER_HEAD * (h + 1)), qa, zero)
            lhs.append(jnp.concatenate([qm, am], axis=1))

        def step(t, carry, masked):
            m0, l0, m1, l1, acc = carry
            ks = pl.multiple_of(t * tq, tq)
            rhs = jnp.concatenate([k_ref[0, pl.ds(ks, tq), sl], ka_ref[0, pl.ds(ks, tq), :]], axis=1)
            v2 = v_ref[0, pl.ds(ks, tq), sl]
            new = []
            for hh, (m_old, l_old) in enumerate(((m0, l0), (m1, l1))):
                s = _dot_nt(lhs[hh], rhs)
                if masked:
                    s = jnp.where(causal, s, -jnp.inf)
                m_new = jnp.maximum(m_old, jnp.max(s, axis=1, keepdims=True))
                alpha = jnp.exp(m_old - m_new)
                p = jnp.exp(s - m_new)
                l_new = alpha * l_old + jnp.sum(p, axis=1, keepdims=True)
                new.append((m_new, l_new, alpha, _dot(p.astype(BF16), v2)))
            acc = acc * jnp.where(lo, new[0][2], new[1][2]) + jnp.where(lo, new[0][3], new[1][3])
            return new[0][0], new[0][1], new[1][0], new[1][1], acc

        col0 = jnp.full((tq, 1), NEG_BIG, F32)
        zc = jnp.zeros((tq, 1), F32)
        init = (col0, zc, col0, zc, jnp.zeros((tq, LANES), F32))
        carry = lax.fori_loop(0, i, functools.partial(step, masked=False), init)
        _, l0, _, l1, acc = step(i, carry, True)
        o_ref[0, :, sl] = (acc / jnp.where(lo, l0, l1)).astype(BF16)


def _attn(q, qa, kb, ka, vb, tq=512):
    b, s, a = q.shape
    tq = min(tq, s)
    tile = lambda w: pl.BlockSpec((1, tq, w), lambda bi, i: (bi, i, 0))
    whole = lambda w: pl.BlockSpec((1, s, w), lambda bi, i: (bi, 0, 0))
    return pl.pallas_call(
        functools.partial(_attn_kernel, tq=tq),
        grid=(b, s // tq),
        in_specs=[tile(a), tile(LANES), whole(a), whole(LANES), whole(a)],
        out_specs=tile(a),
        out_shape=jax.ShapeDtypeStruct((b, s, a), BF16),
        compiler_params=_cparams(("arbitrary", "arbitrary")),
        name="attn_prompt",
    )(q, qa, kb, ka, vb)


def _sconv_kernel(u_ref, st_ref, cw_ref, cb_ref, lg_ref, lb_ref, cv_ref, nst_ref):
    u = u_ref[...]
    st = st_ref[...]
    w = cw_ref[...]
    y = jnp.sum(st * w[0:CONV_K - 1][None, :, :], axis=1) + u * w[CONV_K - 1:CONV_K, :]
    cv_ref[...] = _conv_tail(y, cb_ref[...], lg_ref[...], lb_ref[...]).astype(BF16)
    nst_ref[:, 0:CONV_K - 2, :] = st[:, 1:CONV_K - 1, :]
    nst_ref[:, CONV_K - 2:CONV_K - 1, :] = u[:, None, :]


def _sconv(u, state, cw, cb, lg, lb):
    bd, c = u.shape
    return pl.pallas_call(
        _sconv_kernel,
        out_shape=[jax.ShapeDtypeStruct((bd, c), BF16), jax.ShapeDtypeStruct(state.shape, F32)],
        compiler_params=pltpu.CompilerParams(vmem_limit_bytes=VMEM_LIMIT),
        name="conv_sample",
    )(u, state, cw, cb, lg, lb)


def _decode_kernel(pt_ref, q_ref, kn_ref, vn_ref, lfn_ref, *refs, pp, n_steps):
    k_refs = refs[0:pp]
    v_refs = refs[pp:2 * pp]
    lf_refs = refs[2 * pp:3 * pp]
    o_ref = refs[3 * pp]
    m_scr, l_scr, acc_scr, f_scr = refs[3 * pp + 1:]
    st = pl.program_id(1)
    a = ATTN_WIDTH
    page = k_refs[0].shape[1]

    @pl.when(st == 0)
    def _():
        m_scr[...] = jnp.full_like(m_scr, NEG_BIG)
        l_scr[...] = jnp.zeros_like(l_scr)
        acc_scr[...] = jnp.zeros_like(acc_scr)
        f_scr[...] = jnp.zeros_like(f_scr)

    head_of_lane = lax.broadcasted_iota(jnp.int32, (N_HEADS, a), 1) // HEAD_DIM
    head_row = lax.broadcasted_iota(jnp.int32, (N_HEADS, a), 0)
    diag = head_of_lane == head_row
    qbd = jnp.where(diag, jnp.broadcast_to(q_ref[0].astype(F32), (N_HEADS, a)), 0.0)
    qbd_b = qbd.astype(BF16)

    rowi = lax.broadcasted_iota(jnp.int32, (page, page), 0)
    coli = lax.broadcasted_iota(jnp.int32, (page, page), 1)
    triu = (rowi <= coli).astype(BF16)
    eye8 = (lax.broadcasted_iota(jnp.int32, (N_HEADS, N_HEADS), 0) ==
            lax.broadcasted_iota(jnp.int32, (N_HEADS, N_HEADS), 1))
    eye8b = eye8.astype(BF16)

    fc = f_scr[...]
    s_parts = []
    for r in range(pp):
        lft = jnp.concatenate([_dot_nt(eye8b, part) for part in _split3(lf_refs[r][0])], axis=0)
        p3 = _dot(lft.astype(BF16), triu)
        fpre = (p3[0:N_HEADS] + p3[N_HEADS:2 * N_HEADS] + p3[2 * N_HEADS:]) + fc
        fc = fpre[:, page - 1:page]
        s_parts.append(_dot_nt(qbd_b, k_refs[r][0].astype(BF16)) - fpre)
    f_scr[...] = fc
    s = jnp.concatenate(s_parts, axis=1)
    m_old = m_scr[...]
    m_new = jnp.maximum(m_old, jnp.max(s, axis=1, keepdims=True))
    alpha = jnp.exp(m_old - m_new)
    p = jnp.exp(s - m_new)
    l_new = alpha * l_scr[...] + jnp.sum(p, axis=1, keepdims=True)
    pb = p.astype(BF16)
    acc = acc_scr[...] * alpha
    for r in range(pp):
        acc = acc + _dot(pb[:, r * page:(r + 1) * page], v_refs[r][0].astype(BF16))
    m_scr[...] = m_new
    l_scr[...] = l_new
    acc_scr[...] = acc

    @pl.when(st == n_steps - 1)
    def _():
        f_new = fc + jnp.sum(jnp.where(eye8, jnp.broadcast_to(lfn_ref[0], (N_HEADS, N_HEADS)), 0.0),
                             axis=1, keepdims=True)
        s_self = jnp.sum(qbd * kn_ref[0], axis=1, keepdims=True) - f_new
        m_f = jnp.maximum(m_new, s_self)
        al = jnp.exp(m_new - m_f)
        p_self = jnp.exp(s_self - m_f)
        l_f = l_new * al + p_self
        acc_f = acc * al + p_self * vn_ref[0]
        o = jnp.where(diag, acc_f / l_f, 0.0)
        o_ref[0] = jnp.sum(o, axis=0, keepdims=True).astype(BF16)


def _decode(q, k_new, v_new, logf_new, cache_k, cache_v, cache_logf, page_table, pp=8):
    bd, n_pages = page_table.shape
    a = ATTN_WIDTH
    page = cache_k.shape[1]
    pp = min(pp, n_pages)
    n_steps = n_pages // pp
    row = lambda w: pl.BlockSpec((1, 1, w), lambda b, s, pt: (b, 0, 0))

    def paged(w, r):
        return pl.BlockSpec((1, page, w), lambda b, s, pt: (pt[b, s * pp + r], 0, 0))

    in_specs = ([row(a), row(a), row(a), row(N_HEADS)] +
                [paged(a, r) for r in range(pp)] + [paged(a, r) for r in range(pp)] +
                [paged(N_HEADS, r) for r in range(pp)])
    return pl.pallas_call(
        functools.partial(_decode_kernel, pp=pp, n_steps=n_steps),
        grid_spec=pltpu.PrefetchScalarGridSpec(
            num_scalar_prefetch=1,
            grid=(bd, n_steps),
            in_specs=in_specs,
            out_specs=row(a),
            scratch_shapes=[pltpu.VMEM((N_HEADS, 1), F32), pltpu.VMEM((N_HEADS, 1), F32),
                            pltpu.VMEM((N_HEADS, a), F32), pltpu.VMEM((N_HEADS, 1), F32)]),
        out_shape=jax.ShapeDtypeStruct((bd, 1, a), BF16),
        compiler_params=_cparams(("arbitrary", "arbitrary")),
        name="decode_attn",
    )(page_table, q, k_new, v_new, logf_new, *([cache_k] * pp), *([cache_v] * pp), *([cache_logf] * pp))


def _outproj_kernel(a_ref, cv_ref, x_ref, mod_ref, g_ref, wo_ref, wr_ref, br_ref,
                    xp_ref, t_ref, gate_ref, gt_scr, *, tm):
    aw = a_ref.shape[2]
    m = mod_ref[0]
    y = _dot(a_ref[0], wo_ref[0:aw, :]) + _dot(cv_ref[0], wo_ref[aw:, :])
    xp = x_ref[0] + m[2] * y
    xp_ref[0] = xp
    t = _rms(xp, g_ref[...]) * (1.0 + m[4]) + m[3]
    t_ref[0] = t.astype(BF16)

    th, tl = _split2(t)
    wr = wr_ref[...]
    wh, wl = _split2(wr)
    lg = _dot(th, wh) + (_dot(th, wl) + _dot(tl, wh)) + br_ref[...]
    lt = lg.T
    g = [lt[r:r + 1, :] for r in range(N_GROUPS)]
    gmax = jnp.maximum(jnp.maximum(g[0], g[1]), jnp.maximum(g[2], g[3]))
    gidx = jnp.where(g[0] >= gmax, 0, jnp.where(g[1] >= gmax, 1, jnp.where(g[2] >= gmax, 2, 3)))
    gden = (jnp.exp(g[0] - gmax) + jnp.exp(g[1] - gmax)) + (jnp.exp(g[2] - gmax) + jnp.exp(g[3] - gmax))
    gw = 1.0 / gden
    e = []
    for r in range(EXPERTS_PER_GROUP):
        er = jnp.zeros_like(gmax)
        for gi in range(N_GROUPS):
            c = N_GROUPS + gi * EXPERTS_PER_GROUP + r
            er = jnp.where(gidx == gi, lt[c:c + 1, :], er)
        e.append(er)
    v1 = jnp.maximum(jnp.maximum(e[0], e[1]), jnp.maximum(e[2], e[3]))
    i1 = jnp.where(e[0] >= v1, 0, jnp.where(e[1] >= v1, 1, jnp.where(e[2] >= v1, 2, 3)))
    rest = [jnp.where(i1 == r, -jnp.inf, e[r]) for r in range(EXPERTS_PER_GROUP)]
    v2 = jnp.maximum(jnp.maximum(rest[0], rest[1]), jnp.maximum(rest[2], rest[3]))
    i2 = jnp.where(rest[0] >= v2, 0, jnp.where(rest[1] >= v2, 1, jnp.where(rest[2] >= v2, 2, 3)))
    ex = jnp.exp(v2 - v1)
    w1 = gw / (1.0 + ex)
    w2 = gw * ex / (1.0 + ex)
    id1 = gidx * EXPERTS_PER_GROUP + i1
    id2 = gidx * EXPERTS_PER_GROUP + i2
    gt_scr[...] = jnp.zeros_like(gt_scr)
    for ei in range(N_EXPERTS):
        gt_scr[ei:ei + 1, :] = jnp.where(id1 == ei, w1, 0.0) + jnp.where(id2 == ei, w2, 0.0)
    gate_ref[0] = gt_scr[...].T[:, 0:N_EXPERTS]


def _outproj(a, cv, x, mod, g, wo, wr_pad, br_pad, tm=512):
    b, s, d = x.shape
    tm = min(tm, s)
    aw, cw = a.shape[2], cv.shape[2]
    rmod = mod.shape[2]
    mod_spec = (pl.BlockSpec((1, N_MOD, 1, d), lambda bi, i: (bi, 0, 0, 0)) if rmod == 1 else
                pl.BlockSpec((1, N_MOD, tm, d), lambda bi, i: (bi, 0, i, 0)))
    tile = lambda w: pl.BlockSpec((1, tm, w), lambda bi, i: (bi, i, 0))
    full = lambda arr: pl.BlockSpec(arr.shape, lambda bi, i: (0,) * arr.ndim)
    return pl.pallas_call(
        functools.partial(_outproj_kernel, tm=tm),
        grid=(b, s // tm),
        in_specs=[tile(aw), tile(cw), tile(d), mod_spec, full(g), full(wo), full(wr_pad), full(br_pad)],
        out_specs=[tile(d), tile(d), tile(N_EXPERTS)],
        out_shape=[jax.ShapeDtypeStruct((b, s, d), F32), jax.ShapeDtypeStruct((b, s, d), BF16),
                   jax.ShapeDtypeStruct((b, s, N_EXPERTS), F32)],
        scratch_shapes=[pltpu.VMEM((LANES, tm), F32)],
        compiler_params=_cparams(("arbitrary", "arbitrary")),
        name="outproj_router",
    )(a, cv, x, mod, g, wo, wr_pad, br_pad)


def _moe_kernel(t_ref, gate_ref, xp_ref, mod_ref, modf_ref, gf_ref, w1_ref, w3_ref, w2_ref, y_ref, acc):
    e = pl.program_id(2)

    @pl.when(e == 0)
    def _():
        acc[...] = jnp.zeros_like(acc)

    t = t_ref[0]
    h1 = _dot(t, w1_ref[0])
    he = (h1 * _sigmoid(h1)) * _dot(t, w3_ref[0])
    gate = gate_ref[0]
    lane = lax.broadcasted_iota(jnp.int32, gate.shape, 1)
    gcol = jnp.sum(jnp.where(lane == e, gate, 0.0), axis=1, keepdims=True)
    acc[...] += gcol * _dot(he.astype(BF16), w2_ref[0])

    @pl.when(e == pl.num_programs(2) - 1)
    def _():
        xp2 = xp_ref[0] + mod_ref[0][5] * acc[...]
        mf = modf_ref[0]
        y_ref[0] = _rms(xp2, gf_ref[...]) * (1.0 + mf[1]) + mf[0]


def _moe(t, gate, xp, mod, modf, gf, w1, w3, w2, tm=512):
    b, s, d = xp.shape
    tm = min(tm, s)
    ne, _, de = w1.shape
    rmod = mod.shape[2]

    def mod_spec(n):
        return (pl.BlockSpec((1, n, 1, d), lambda bi, i, e: (bi, 0, 0, 0)) if rmod == 1 else
                pl.BlockSpec((1, n, tm, d), lambda bi, i, e: (bi, 0, i, 0)))

    tile = lambda w: pl.BlockSpec((1, tm, w), lambda bi, i, e: (bi, i, 0))
    return pl.pallas_call(
        _moe_kernel,
        grid=(b, s // tm, ne),
        in_specs=[tile(d), tile(ne), tile(d), mod_spec(N_MOD), mod_spec(2),
                  pl.BlockSpec(gf.shape, lambda bi, i, e: (0, 0)),
                  pl.BlockSpec((1, d, de), lambda bi, i, e: (e, 0, 0)),
                  pl.BlockSpec((1, d, de), lambda bi, i, e: (e, 0, 0)),
                  pl.BlockSpec((1, de, d), lambda bi, i, e: (e, 0, 0))],
        out_specs=tile(d),
        out_shape=jax.ShapeDtypeStruct((b, s, d), F32),
        scratch_shapes=[pltpu.VMEM((tm, d), F32)],
        compiler_params=_cparams(("arbitrary", "arbitrary", "arbitrary")),
        name="moe_final",
    )(t, gate, xp, mod, modf, gf, w1, w3, w2)


def _prep_w_in(w_in, b_f):
    a = ATTN_WIDTH
    d = w_in.shape[0]
    cwid = (w_in.shape[1] - 3 * a - N_HEADS) // 2
    wf = w_in[:, 3 * a:3 * a + N_HEADS]
    lane_head = jnp.concatenate([jnp.repeat(jnp.arange(N_HEADS), AUG_PER_HEAD), jnp.arange(N_HEADS)])
    npad = LANES - lane_head.shape[0]
    wf_pad = jnp.concatenate([wf[:, lane_head], jnp.zeros((d, npad), F32)], axis=1)
    bf_pad = jnp.concatenate([b_f[lane_head], jnp.zeros((npad,), F32)]).reshape(1, LANES)
    w_cat = jnp.concatenate([w_in[:, 0:3 * a], w_in[:, 3 * a + N_HEADS:], wf_pad], axis=1).astype(BF16)
    del cwid
    return w_cat, bf_pad


def kernel(x_prompt, x_sample, cache_k, cache_v, cache_logf, state_conv, page_table, c_prompt, c_sample, w_ada, b_ada, g_norm1, g_norm2, w_in, b_f, conv_w, conv_b, conv_ln_g, conv_ln_b, w_out, w_rg, b_rg, w_re, b_re, w1, w3, w2, w_ada_f, b_ada_f, g_final):
    depth = w_ada.shape[0]
    assert depth == 1
    b, s, d = x_prompt.shape
    bd, ds, _ = x_sample.shape
    assert ds == 1
    a = ATTN_WIDTH
    n_phys, page = cache_k.shape[1], cache_k.shape[2]

    c_all = jnp.concatenate([c_prompt, c_sample], axis=0)
    mod = _ada(c_all, w_ada[0], b_ada[0]).reshape(b + bd, N_MOD, d)
    modf = _ada(c_all, w_ada_f, b_ada_f).reshape(b + bd, 2, d)
    mp = mod[:b].reshape(b, N_MOD, 1, d)
    ms = mod[b:].transpose(1, 0, 2).reshape(1, N_MOD, bd, d)
    mfp = modf[:b].reshape(b, 2, 1, d)
    mfs = modf[b:].transpose(1, 0, 2).reshape(1, 2, bd, d)

    w_cat, bf_pad = _prep_w_in(w_in[0], b_f[0])
    row = lambda v: v.reshape(1, -1)
    g1, g2, gf = row(g_norm1[0]), row(g_norm2[0]), row(g_final)
    cw, cb, lg, lb = conv_w[0], row(conv_b[0]), row(conv_ln_g[0]), row(conv_ln_b[0])
    wo = w_out[0].astype(BF16)
    nr = N_GROUPS + N_EXPERTS
    wr_pad = jnp.concatenate([w_rg[0], w_re[0], jnp.zeros((d, LANES - nr), F32)], axis=1)
    br_pad = jnp.concatenate([b_rg[0], b_re[0], jnp.zeros((LANES - nr,), F32)]).reshape(1, LANES)
    w1b, w3b, w2b = w1[0].astype(BF16), w3[0].astype(BF16), w2[0].astype(BF16)

    q, k_p, v_p, lf_p, kb, vb, qa, ka, cv_p, st_p = _inproj(x_prompt, mp, g1, w_cat, bf_pad, conv=(cw, cb, lg, lb))
    a_p = _attn(q, qa, kb, ka, vb)
    xp1, t_p, gate_p = _outproj(a_p, cv_p, x_prompt, mp, g2, wo, wr_pad, br_pad)
    y_prompt = _moe(t_p, gate_p, xp1, mp, mfp, gf, w1b, w3b, w2b)

    xs = x_sample.reshape(1, bd, d)
    q_s, k_s, v_s, lf_s, u_s = _inproj(xs, ms, g1, w_cat, bf_pad)
    cv_s, st_s = _sconv(u_s[0], state_conv[0], cw, cb, lg, lb)
    a_s = _decode(q_s.reshape(bd, 1, a), k_s.reshape(bd, 1, a), v_s.reshape(bd, 1, a),
                  lf_s.reshape(bd, 1, N_HEADS),
                  cache_k[0].reshape(n_phys, page, a), cache_v[0].reshape(n_phys, page, a),
                  cache_logf[0], page_table)
    xs1, t_s, gate_s = _outproj(a_s.reshape(1, bd, a), cv_s.reshape(1, bd, -1), xs, ms, g2, wo, wr_pad, br_pad)
    y_sample = _moe(t_s, gate_s, xs1, ms, mfs, gf, w1b, w3b, w2b).reshape(bd, 1, d)

    hs = (N_HEADS, HEAD_DIM)
    return (y_prompt, y_sample,
            k_p.reshape(1, b, s, *hs), v_p.reshape(1, b, s, *hs), lf_p.reshape(1, b, s, N_HEADS), st_p[None],
            k_s.reshape(1, bd, 1, *hs), v_s.reshape(1, bd, 1, *hs), lf_s.reshape(1, bd, 1, N_HEADS), st_s[None])
```

```python
import functools

import jax
import jax.numpy as jnp
from jax import lax
from jax.experimental import pallas as pl
from jax.experimental.pallas import tpu as pltpu

F32 = jnp.float32
BF16 = jnp.bfloat16

EPS = 1e-6
N_HEADS = 8
HEAD_DIM = 64
ATTN_WIDTH = N_HEADS * HEAD_DIM
CONV_K = 31
N_GROUPS = 4
EXPERTS_PER_GROUP = 4
N_EXPERTS = N_GROUPS * EXPERTS_PER_GROUP
N_MOD = 6
LANES = 128
SUBLANES = 8
N_PAIRS = 6
N_CLASSES = N_GROUPS * N_PAIRS
CLASS_ROWS = 32
MOE_TILE = 512
AUG_PER_HEAD = 6
LOGF_LANE0 = AUG_PER_HEAD * N_HEADS
CONV_HALO = 32
NEG_BIG = -1e30
VMEM_LIMIT = 56 * 1024 * 1024


def _dot(a, b):
    return jnp.dot(a, b, preferred_element_type=F32)


def _dot_nt(a, b):
    return lax.dot_general(a, b, (((1,), (1,)), ((), ())), preferred_element_type=F32)


def _split2(x):
    hi = x.astype(BF16)
    lo = (x - hi.astype(F32)).astype(BF16)
    return hi, lo


def _split3(x):
    hi = x.astype(BF16)
    r1 = x - hi.astype(F32)
    mid = r1.astype(BF16)
    lo = (r1 - mid.astype(F32)).astype(BF16)
    return hi, mid, lo


def _dot_hp(a, b):
    ah, al = _split2(a)
    bh, bl = _split2(b)
    return _dot(ah, bh) + (_dot(ah, bl) + _dot(al, bh))


def _sigmoid(x):
    return 1.0 / (1.0 + jnp.exp(-x))


def _log_sigmoid(x):
    return jnp.minimum(x, 0.0) - jnp.log1p(jnp.exp(-jnp.abs(x)))


def _rms(x, g):
    return x * lax.rsqrt(jnp.mean(x * x, axis=-1, keepdims=True) + EPS) * g


def _cparams(sem):
    return pltpu.CompilerParams(dimension_semantics=sem, vmem_limit_bytes=VMEM_LIMIT)


def _ada_kernel(c_ref, w_ref, b_ref, o_ref):
    c = c_ref[...]
    o_ref[...] = _dot_hp(c * _sigmoid(c), w_ref[...]) + b_ref[...]


def _ada(c, w, b, bn=1024):
    r, d = c.shape
    n = w.shape[1]
    return pl.pallas_call(
        _ada_kernel,
        grid=(n // bn,),
        in_specs=[pl.BlockSpec((r, d), lambda j: (0, 0)),
                  pl.BlockSpec((d, bn), lambda j: (0, j)),
                  pl.BlockSpec((1, bn), lambda j: (0, j))],
        out_specs=pl.BlockSpec((r, bn), lambda j: (0, j)),
        out_shape=jax.ShapeDtypeStruct((r, n), F32),
        compiler_params=_cparams(("arbitrary",)),
        name="ada",
    )(c, w, b.reshape(1, n))


def _conv_tail(y, conv_b, ln_g, ln_b):
    yf = y + conv_b
    mu = jnp.mean(yf, axis=-1, keepdims=True)
    d = yf - mu
    yn = d * lax.rsqrt(jnp.mean(d * d, axis=-1, keepdims=True) + EPS) * ln_g + ln_b
    return yn * _sigmoid(yn)


def _inproj_kernel(*refs, tm, n_i, prompt, conv_rows):
    if prompt:
        (x_ref, mod_ref, g_ref, w_ref, bf_ref, cw_ref, cb_ref, lg_ref, lb_ref,
         q_ref, k_ref, v_ref, lf_ref, kb_ref, vb_ref, qa_ref, ka_ref, cv_ref, st_ref,
         ubuf, fcarry) = refs
    else:
        (x_ref, mod_ref, g_ref, w_ref, bf_ref,
         q_ref, k_ref, v_ref, lf_ref, u_ref) = refs
    i = pl.program_id(1)
    a = ATTN_WIDTH
    x = x_ref[0]
    m = mod_ref[0]
    hb = (_rms(x, g_ref[...]) * (1.0 + m[1]) + m[0]).astype(BF16)

    q_ref[0] = (_dot(hb, w_ref[:, 0:a]) * (HEAD_DIM ** -0.5)).astype(BF16)
    k = _dot(hb, w_ref[:, a:2 * a])
    k_ref[0] = k
    v = _dot(hb, w_ref[:, 2 * a:3 * a])
    v_ref[0] = v
    cwid = (w_ref.shape[1] - 3 * a - LANES) // 2
    o = 3 * a
    u = _dot(hb, w_ref[:, o:o + cwid]) * _sigmoid(_dot(hb, w_ref[:, o + cwid:o + 2 * cwid]))
    lf = _log_sigmoid(_dot(hb, w_ref[:, o + 2 * cwid:]) + bf_ref[...])
    lf_ref[0] = lf[:, LOGF_LANE0:LOGF_LANE0 + N_HEADS]
    if not prompt:
        u_ref[0] = u
        return

    kb_ref[0] = k.astype(BF16)
    vb_ref[0] = v.astype(BF16)

    @pl.when(i == 0)
    def _():
        ubuf[0, 0:CONV_HALO, :] = jnp.zeros((CONV_HALO, cwid), F32)
        fcarry[...] = jnp.zeros_like(fcarry)

    cb = 256 if tm % 256 == 0 else tm
    row = lax.broadcasted_iota(jnp.int32, (cb, cb), 0)
    col = lax.broadcasted_iota(jnp.int32, (cb, cb), 1)
    tri = (row >= col).astype(BF16)
    parts = jnp.concatenate(_split3(lf), axis=1)
    carry = fcarry[...]
    lane = lax.broadcasted_iota(jnp.int32, (1, LANES), 1)
    r = lane % AUG_PER_HEAD
    valid = lane < LOGF_LANE0
    for c in range(tm // cb):
        p3 = _dot(tri, parts[c * cb:(c + 1) * cb, :])
        fc = (p3[:, 0:LANES] + p3[:, LANES:2 * LANES] + p3[:, 2 * LANES:]) + carry
        carry = fc[cb - 1:cb, :]
        fh = fc.astype(BF16).astype(F32)
        r1 = fc - fh
        fm = r1.astype(BF16).astype(F32)
        fl = (r1 - fm).astype(BF16).astype(F32)
        qa = jnp.where(r == 0, fh, jnp.where(r == 1, fm, jnp.where(r == 2, fl, 1.0)))
        ka = jnp.where(r < 3, 1.0, jnp.where(r == 3, -fh, jnp.where(r == 4, -fm, -fl)))
        qa_ref[0, c * cb:(c + 1) * cb, :] = jnp.where(valid, qa, 0.0).astype(BF16)
        ka_ref[0, c * cb:(c + 1) * cb, :] = jnp.where(valid, ka, 0.0).astype(BF16)
    fcarry[...] = carry

    nb = tm + CONV_HALO
    ubuf[0, CONV_HALO:nb, :] = u
    for r in range(1, SUBLANES):
        ubuf[r, SUBLANES:nb, :] = ubuf[0, SUBLANES - r:nb - r, :]
    for c in range(tm // conv_rows):
        base = CONV_HALO + c * conv_rows
        acc = cw_ref[CONV_K - 1:CONV_K, :] * ubuf[0, base:base + conv_rows, :]
        for d in range(1, CONV_K):
            a8, r = (d // SUBLANES) * SUBLANES, d % SUBLANES
            acc = acc + cw_ref[CONV_K - 1 - d:CONV_K - d, :] * ubuf[r, base - a8:base - a8 + conv_rows, :]
        cv_ref[0, c * conv_rows:(c + 1) * conv_rows, :] = _conv_tail(
            acc, cb_ref[...], lg_ref[...], lb_ref[...]).astype(BF16)

    @pl.when(i == n_i - 1)
    def _():
        st_ref[0] = ubuf[0, nb - (CONV_K - 1):nb, :]

    ubuf[0, 0:CONV_HALO, :] = ubuf[0, tm:nb, :]


def _inproj(x, mod, g, w_cat, bf_pad, conv=None, tm=512):
    b, s, d = x.shape
    tm = min(tm, s)
    n_i = s // tm
    a = ATTN_WIDTH
    cwid = (w_cat.shape[1] - 3 * a - LANES) // 2
    rmod = mod.shape[2]
    mod_spec = (pl.BlockSpec((1, N_MOD, 1, d), lambda bi, i: (bi, 0, 0, 0)) if rmod == 1 else
                pl.BlockSpec((1, N_MOD, tm, d), lambda bi, i: (bi, 0, i, 0)))
    tile = lambda w: pl.BlockSpec((1, tm, w), lambda bi, i: (bi, i, 0))
    full = lambda arr: pl.BlockSpec(arr.shape, lambda bi, i: (0,) * arr.ndim)
    prompt = conv is not None
    ins = [x, mod, g, w_cat, bf_pad]
    in_specs = [tile(d), mod_spec, full(g), full(w_cat), full(bf_pad)]
    outs = [jax.ShapeDtypeStruct((b, s, a), BF16), jax.ShapeDtypeStruct((b, s, a), F32),
            jax.ShapeDtypeStruct((b, s, a), F32), jax.ShapeDtypeStruct((b, s, N_HEADS), F32)]
    out_specs = [tile(a), tile(a), tile(a), tile(N_HEADS)]
    scratch = []
    if prompt:
        ins += list(conv)
        in_specs += [full(c) for c in conv]
        outs += [jax.ShapeDtypeStruct((b, s, a), BF16), jax.ShapeDtypeStruct((b, s, a), BF16),
                 jax.ShapeDtypeStruct((b, s, LANES), BF16), jax.ShapeDtypeStruct((b, s, LANES), BF16),
                 jax.ShapeDtypeStruct((b, s, cwid), BF16), jax.ShapeDtypeStruct((b, CONV_K - 1, cwid), F32)]
        out_specs += [tile(a), tile(a), tile(LANES), tile(LANES), tile(cwid),
                      pl.BlockSpec((1, CONV_K - 1, cwid), lambda bi, i: (bi, 0, 0))]
        scratch = [pltpu.VMEM((SUBLANES, tm + CONV_HALO, cwid), F32), pltpu.VMEM((1, LANES), F32)]
    else:
        outs += [jax.ShapeDtypeStruct((b, s, cwid), F32)]
        out_specs += [tile(cwid)]
    return pl.pallas_call(
        functools.partial(_inproj_kernel, tm=tm, n_i=n_i, prompt=prompt, conv_rows=min(32, tm)),
        grid=(b, n_i),
        in_specs=in_specs,
        out_specs=out_specs,
        out_shape=outs,
        scratch_shapes=scratch,
        compiler_params=_cparams(("arbitrary", "arbitrary")),
        name="inproj_prompt" if prompt else "inproj_sample",
    )(*ins)


def _attn_kernel(q_ref, qa_ref, k_ref, ka_ref, v_ref, o_ref, lhs_scr, *, tq):
    i = pl.program_id(1)
    lane = lax.broadcasted_iota(jnp.int32, (1, LANES), 1)
    lo = lane < HEAD_DIM
    qa = qa_ref[0]
    zero = jnp.zeros((), BF16)
    n_slabs = N_HEADS // 2

    for h in range(N_HEADS):
        q2 = q_ref[0, :, (h // 2) * LANES:(h // 2 + 1) * LANES]
        lhs_scr[h, :, 0:LANES] = jnp.where(lo if h % 2 == 0 else jnp.logical_not(lo), q2, zero)
        lhs_scr[h, :, LANES:] = jnp.where((lane >= AUG_PER_HEAD * h) & (lane < AUG_PER_HEAD * (h + 1)), qa, zero)

    def step(t, carry, masked):
        ms, ls, accs = carry
        ks = pl.multiple_of(t * tq, tq)
        kaug = ka_ref[0, pl.ds(ks, tq), :]
        ones = jnp.ones((tq, LANES), BF16)
        if masked:
            causal = (lax.broadcasted_iota(jnp.int32, (tq, tq), 1) <=
                      lax.broadcasted_iota(jnp.int32, (tq, tq), 0))
        new_m, new_l, new_acc = [], [], []
        for j in range(n_slabs):
            sl = slice(j * LANES, (j + 1) * LANES)
            rhs = jnp.concatenate([k_ref[0, pl.ds(ks, tq), sl], kaug], axis=1)
            v2 = jnp.concatenate([v_ref[0, pl.ds(ks, tq), sl], ones], axis=1)
            alphas, pvs = [], []
            for h in (2 * j, 2 * j + 1):
                s = _dot_nt(lhs_scr[h], rhs)
                if masked:
                    s = jnp.where(causal, s, -jnp.inf)
                m_new = jnp.maximum(ms[h], jnp.max(s, axis=1, keepdims=True))
                alphas.append(jnp.exp(ms[h] - m_new))
                new_m.append(m_new)
                pvs.append(_dot(jnp.exp(s - m_new).astype(BF16), v2))
            alpha2 = jnp.where(lo, alphas[0], alphas[1])
            new_acc.append(accs[j] * alpha2 + jnp.where(lo, pvs[0][:, 0:LANES], pvs[1][:, 0:LANES]))
            new_l.append(ls[j] * alpha2 + jnp.where(lo, pvs[0][:, LANES:], pvs[1][:, LANES:]))
        return tuple(new_m), tuple(new_l), tuple(new_acc)

    init = (tuple(jnp.full((tq, 1), NEG_BIG, F32) for _ in range(N_HEADS)),
            tuple(jnp.zeros((tq, LANES), F32) for _ in range(n_slabs)),
            tuple(jnp.zeros((tq, LANES), F32) for _ in range(n_slabs)))
    carry = lax.fori_loop(0, i, functools.partial(step, masked=False), init)
    _, ls, accs = step(i, carry, True)
    for j in range(n_slabs):
        o_ref[0, :, j * LANES:(j + 1) * LANES] = (accs[j] / ls[j]).astype(BF16)


def _attn(q, qa, kb, ka, vb, tq=512):
    b, s, a = q.shape
    tq = min(tq, s)
    tile = lambda w: pl.BlockSpec((1, tq, w), lambda bi, i: (bi, i, 0))
    whole = lambda w: pl.BlockSpec((1, s, w), lambda bi, i: (bi, 0, 0))
    return pl.pallas_call(
        functools.partial(_attn_kernel, tq=tq),
        grid=(b, s // tq),
        in_specs=[tile(a), tile(LANES), whole(a), whole(LANES), whole(a)],
        out_specs=tile(a),
        out_shape=jax.ShapeDtypeStruct((b, s, a), BF16),
        scratch_shapes=[pltpu.VMEM((N_HEADS, tq, 2 * LANES), BF16)],
        compiler_params=_cparams(("arbitrary", "arbitrary")),
        name="attn_prompt",
    )(q, qa, kb, ka, vb)


def _sconv_kernel(u_ref, st_ref, cw_ref, cb_ref, lg_ref, lb_ref, cv_ref, nst_ref):
    u = u_ref[...]
    y = u * cw_ref[CONV_K - 1:CONV_K, :]
    for j in range(CONV_K - 1):
        y = y + st_ref[j] * cw_ref[j:j + 1, :]
    cv_ref[...] = _conv_tail(y, cb_ref[...], lg_ref[...], lb_ref[...]).astype(BF16)
    for j in range(CONV_K - 2):
        nst_ref[j] = st_ref[j + 1]
    nst_ref[CONV_K - 2] = u


def _sconv(u, state, cw, cb, lg, lb):
    bd, c = u.shape
    return pl.pallas_call(
        _sconv_kernel,
        out_shape=[jax.ShapeDtypeStruct((bd, c), BF16), jax.ShapeDtypeStruct(state.shape, F32)],
        compiler_params=pltpu.CompilerParams(vmem_limit_bytes=VMEM_LIMIT),
        name="conv_sample",
    )(u, state, cw, cb, lg, lb)


def _decode_kernel(pt_ref, q_ref, kn_ref, vn_ref, lfn_ref, *refs, pp, n_steps):
    kt_refs = refs[0:pp]
    vt_refs = refs[pp:2 * pp]
    lf_refs = refs[2 * pp:3 * pp]
    o_ref = refs[3 * pp]
    m_scr, l_scr, acc_scr, f_scr = refs[3 * pp + 1:]
    st = pl.program_id(1)
    a = ATTN_WIDTH
    page = kt_refs[0].shape[2]

    @pl.when(st == 0)
    def _():
        m_scr[...] = jnp.full_like(m_scr, NEG_BIG)
        l_scr[...] = jnp.zeros_like(l_scr)
        acc_scr[...] = jnp.zeros_like(acc_scr)
        f_scr[...] = jnp.zeros_like(f_scr)

    head_of_lane = lax.broadcasted_iota(jnp.int32, (N_HEADS, a), 1) // HEAD_DIM
    head_row = lax.broadcasted_iota(jnp.int32, (N_HEADS, a), 0)
    diag = head_of_lane == head_row
    qbd = jnp.where(diag, jnp.broadcast_to(q_ref[0].astype(F32), (N_HEADS, a)), 0.0)
    qbd_b = qbd.astype(BF16)

    rowi = lax.broadcasted_iota(jnp.int32, (page, page), 0)
    coli = lax.broadcasted_iota(jnp.int32, (page, page), 1)
    triu = (rowi <= coli).astype(BF16)
    fc = f_scr[...]
    s_parts = []
    for r in range(pp):
        lf = lf_refs[r][0]
        p3 = _dot(jnp.concatenate(_split3(lf), axis=0), triu)
        fpre = (p3[0:N_HEADS] + p3[N_HEADS:2 * N_HEADS] + p3[2 * N_HEADS:]) + fc
        fc = fc + jnp.sum(lf, axis=1, keepdims=True)
        s_parts.append(_dot(qbd_b, kt_refs[r][0].astype(BF16)) - fpre)
    f_scr[...] = fc
    s = jnp.concatenate(s_parts, axis=1)
    m_old = m_scr[...]
    m_new = jnp.maximum(m_old, jnp.max(s, axis=1, keepdims=True))
    alpha = jnp.exp(m_old - m_new)
    p = jnp.exp(s - m_new)
    l_new = alpha * l_scr[...] + jnp.sum(p, axis=1, keepdims=True)
    pb = p.astype(BF16)
    acc = acc_scr[...] * alpha
    for r in range(pp):
        acc = acc + _dot_nt(pb[:, r * page:(r + 1) * page], vt_refs[r][0].astype(BF16))
    m_scr[...] = m_new
    l_scr[...] = l_new
    acc_scr[...] = acc

    @pl.when(st == n_steps - 1)
    def _():
        eye8 = (lax.broadcasted_iota(jnp.int32, (N_HEADS, N_HEADS), 0) ==
                lax.broadcasted_iota(jnp.int32, (N_HEADS, N_HEADS), 1))
        f_new = fc + jnp.sum(jnp.where(eye8, jnp.broadcast_to(lfn_ref[0], (N_HEADS, N_HEADS)), 0.0),
                             axis=1, keepdims=True)
        s_self = jnp.sum(qbd * kn_ref[0], axis=1, keepdims=True) - f_new
        m_f = jnp.maximum(m_new, s_self)
        al = jnp.exp(m_new - m_f)
        p_self = jnp.exp(s_self - m_f)
        l_f = l_new * al + p_self
        acc_f = acc * al + p_self * vn_ref[0]
        o = jnp.where(diag, acc_f / l_f, 0.0)
        o_ref[0] = jnp.sum(o, axis=0, keepdims=True).astype(BF16)


def _decode(q, k_new, v_new, logf_new, cache_k, cache_v, cache_logf, page_table, pp=16):
    bd, n_pages = page_table.shape
    a = ATTN_WIDTH
    page = cache_k.shape[2]
    pp = min(pp, n_pages)
    n_steps = n_pages // pp
    row = lambda w: pl.BlockSpec((1, 1, w), lambda b, s, pt: (b, 0, 0))

    def paged(w, r):
        return pl.BlockSpec((1, w, page), lambda b, s, pt: (pt[b, s * pp + r], 0, 0))

    in_specs = ([row(a), row(a), row(a), row(N_HEADS)] +
                [paged(a, r) for r in range(pp)] + [paged(a, r) for r in range(pp)] +
                [paged(N_HEADS, r) for r in range(pp)])
    return pl.pallas_call(
        functools.partial(_decode_kernel, pp=pp, n_steps=n_steps),
        grid_spec=pltpu.PrefetchScalarGridSpec(
            num_scalar_prefetch=1,
            grid=(bd, n_steps),
            in_specs=in_specs,
            out_specs=row(a),
            scratch_shapes=[pltpu.VMEM((N_HEADS, 1), F32), pltpu.VMEM((N_HEADS, 1), F32),
                            pltpu.VMEM((N_HEADS, a), F32), pltpu.VMEM((N_HEADS, 1), F32)]),
        out_shape=jax.ShapeDtypeStruct((bd, 1, a), BF16),
        compiler_params=_cparams(("arbitrary", "arbitrary")),
        name="decode_attn",
    )(page_table, q, k_new, v_new, logf_new, *([cache_k] * pp), *([cache_v] * pp), *([cache_logf] * pp))


def _outproj_kernel(a_ref, cv_ref, x_ref, mod_ref, g_ref, wo_ref, wr_ref, br_ref, *refs, tm, sparse):
    if sparse:
        xp_ref, t_ref, ri_ref, cnt_ref, gt_scr, cnt_scr = refs
    else:
        xp_ref, t_ref, gate_ref, gt_scr = refs
    aw = a_ref.shape[2]
    d = x_ref.shape[2]
    m = mod_ref[0]
    y = _dot(a_ref[0], wo_ref[0:aw, :]) + _dot(cv_ref[0], wo_ref[aw:, :])
    xp = x_ref[0] + m[2] * y
    xp_ref[0] = xp
    t = _rms(xp, g_ref[...]) * (1.0 + m[4]) + m[3]
    if sparse:
        t_ref[0, :, 0:d] = t
    else:
        t_ref[0] = t.astype(BF16)

    th, tl = _split2(t)
    wr = wr_ref[...]
    wh, wl = _split2(wr)
    lg = _dot(th, wh) + (_dot(th, wl) + _dot(tl, wh)) + br_ref[...]
    lt = lg.T
    g = [lt[r:r + 1, :] for r in range(N_GROUPS)]
    gmax = jnp.maximum(jnp.maximum(g[0], g[1]), jnp.maximum(g[2], g[3]))
    gidx = jnp.where(g[0] >= gmax, 0, jnp.where(g[1] >= gmax, 1, jnp.where(g[2] >= gmax, 2, 3)))
    gden = (jnp.exp(g[0] - gmax) + jnp.exp(g[1] - gmax)) + (jnp.exp(g[2] - gmax) + jnp.exp(g[3] - gmax))
    gw = 1.0 / gden
    e = []
    for r in range(EXPERTS_PER_GROUP):
        er = jnp.zeros_like(gmax)
        for gi in range(N_GROUPS):
            c = N_GROUPS + gi * EXPERTS_PER_GROUP + r
            er = jnp.where(gidx == gi, lt[c:c + 1, :], er)
        e.append(er)
    v1 = jnp.maximum(jnp.maximum(e[0], e[1]), jnp.maximum(e[2], e[3]))
    i1 = jnp.where(e[0] >= v1, 0, jnp.where(e[1] >= v1, 1, jnp.where(e[2] >= v1, 2, 3)))
    rest = [jnp.where(i1 == r, -jnp.inf, e[r]) for r in range(EXPERTS_PER_GROUP)]
    v2 = jnp.maximum(jnp.maximum(rest[0], rest[1]), jnp.maximum(rest[2], rest[3]))
    i2 = jnp.where(rest[0] >= v2, 0, jnp.where(rest[1] >= v2, 1, jnp.where(rest[2] >= v2, 2, 3)))
    ex = jnp.exp(v2 - v1)
    w1 = gw / (1.0 + ex)
    w2 = gw * ex / (1.0 + ex)
    gt_scr[...] = jnp.zeros_like(gt_scr)
    if not sparse:
        id1 = gidx * EXPERTS_PER_GROUP + i1
        id2 = gidx * EXPERTS_PER_GROUP + i2
        for ei in range(N_EXPERTS):
            gt_scr[ei:ei + 1, :] = jnp.where(id1 == ei, w1, 0.0) + jnp.where(id2 == ei, w2, 0.0)
        gate_ref[0] = gt_scr[...].T[:, 0:N_EXPERTS]
        return

    first = i1 < i2
    lo = jnp.minimum(i1, i2)
    hi = jnp.maximum(i1, i2)
    pair = jnp.where(lo == 0, hi - 1, jnp.where(lo == 1, hi + 1, N_PAIRS - 1))
    cls = gidx * N_PAIRS + pair
    gt_scr[0:1, :] = jnp.where(first, w1, w2)
    gt_scr[1:2, :] = jnp.where(first, w2, w1)
    t_ref[0, :, d:] = gt_scr[...].T

    @pl.when((pl.program_id(0) == 0) & (pl.program_id(1) == 0))
    def _():
        cnt_scr[...] = jnp.zeros_like(cnt_scr)

    oh = (lax.broadcasted_iota(jnp.int32, (CLASS_ROWS, tm), 0) == cls).astype(F32)
    before = (lax.broadcasted_iota(jnp.int32, (tm, tm), 0) <
              lax.broadcasted_iota(jnp.int32, (tm, tm), 1)).astype(BF16)
    carry = cnt_scr[...]
    pref = _dot(oh.astype(BF16), before) + carry[:, 0:1]
    rank = jnp.sum(oh * pref, axis=0, keepdims=True)
    ri_ref[0] = jnp.zeros(ri_ref.shape[1:], jnp.int32)
    ri_ref[0, 0:1, :] = cls
    ri_ref[0, 1:2, :] = rank.astype(jnp.int32)
    total = carry + jnp.sum(oh, axis=1, keepdims=True)
    cnt_scr[...] = total
    cnt_ref[...] = total.astype(jnp.int32)


def _outproj(a, cv, x, mod, g, wo, wr_pad, br_pad, sparse, tm=512):
    b, s, d = x.shape
    tm = min(tm, s)
    aw, cw = a.shape[2], cv.shape[2]
    rmod = mod.shape[2]
    mod_spec = (pl.BlockSpec((1, N_MOD, 1, d), lambda bi, i: (bi, 0, 0, 0)) if rmod == 1 else
                pl.BlockSpec((1, N_MOD, tm, d), lambda bi, i: (bi, 0, i, 0)))
    tile = lambda w: pl.BlockSpec((1, tm, w), lambda bi, i: (bi, i, 0))
    full = lambda arr: pl.BlockSpec(arr.shape, lambda bi, i: (0,) * arr.ndim)
    if sparse:
        out_specs = [tile(d), tile(d + LANES), pl.BlockSpec((1, SUBLANES, tm), lambda bi, i: (bi, 0, i)),
                     pl.BlockSpec((CLASS_ROWS, LANES), lambda bi, i: (0, 0))]
        out_shape = [jax.ShapeDtypeStruct((b, s, d), F32), jax.ShapeDtypeStruct((b, s, d + LANES), F32),
                     jax.ShapeDtypeStruct((b, SUBLANES, s), jnp.int32),
                     jax.ShapeDtypeStruct((CLASS_ROWS, LANES), jnp.int32)]
        scratch = [pltpu.VMEM((LANES, tm), F32), pltpu.VMEM((CLASS_ROWS, LANES), F32)]
    else:
        out_specs = [tile(d), tile(d), tile(N_EXPERTS)]
        out_shape = [jax.ShapeDtypeStruct((b, s, d), F32), jax.ShapeDtypeStruct((b, s, d), BF16),
                     jax.ShapeDtypeStruct((b, s, N_EXPERTS), F32)]
        scratch = [pltpu.VMEM((LANES, tm), F32)]
    return pl.pallas_call(
        functools.partial(_outproj_kernel, tm=tm, sparse=sparse),
        grid=(b, s // tm),
        in_specs=[tile(aw), tile(cw), tile(d), mod_spec, full(g), full(wo), full(wr_pad), full(br_pad)],
        out_specs=out_specs,
        out_shape=out_shape,
        scratch_shapes=scratch,
        compiler_params=_cparams(("arbitrary", "arbitrary")),
        name="outproj_router_sparse" if sparse else "outproj_router_dense",
    )(a, cv, x, mod, g, wo, wr_pad, br_pad)


def _dispatch_kernel(zf_ref, dest_ref, src_ref, out_ref, zbuf, sem, *, td, tme, n_tiles):
    i = pl.program_id(0)

    def zero_copy(j):
        return pltpu.make_async_copy(zbuf, out_ref.at[pl.ds(pl.multiple_of(j * tme, tme), tme)], sem.at[0])

    @pl.when(i == 0)
    def _():
        zbuf[...] = jnp.zeros_like(zbuf)

        def start(j, c):
            @pl.when(zf_ref[j] == 1)
            def _():
                zero_copy(j).start()
            return c

        def wait(j, c):
            @pl.when(zf_ref[j] == 1)
            def _():
                zero_copy(j).wait()
            return c

        lax.fori_loop(0, n_tiles, start, 0)
        lax.fori_loop(0, n_tiles, wait, 0)

    def row_copy(c, k):
        r0 = pl.multiple_of(c * SUBLANES, SUBLANES)
        return pltpu.make_async_copy(src_ref.at[pl.ds(r0 + k, 1)], out_ref.at[pl.ds(dest_ref[0, 0, r0 + k], 1)],
                                     sem.at[1])

    def start_rows(c, carry):
        for k in range(SUBLANES):
            row_copy(c, k).start()
        return carry

    def wait_rows(c, carry):
        for k in range(SUBLANES):
            row_copy(c, k).wait()
        return carry

    lax.fori_loop(0, td // SUBLANES, start_rows, 0)
    lax.fori_loop(0, td // SUBLANES, wait_rows, 0)


def _dispatch(rows, dest, zero_flag, n_tiles, tme, td=512):
    n, w = rows.shape
    td = min(td, n)
    return pl.pallas_call(
        functools.partial(_dispatch_kernel, td=td, tme=tme, n_tiles=n_tiles),
        grid_spec=pltpu.PrefetchScalarGridSpec(
            num_scalar_prefetch=1,
            grid=(n // td,),
            in_specs=[pl.BlockSpec((1, 1, td), lambda i, zf: (i, 0, 0), memory_space=pltpu.SMEM),
                      pl.BlockSpec((td, w), lambda i, zf: (i, 0))],
            out_specs=pl.BlockSpec(memory_space=pl.ANY),
            scratch_shapes=[pltpu.VMEM((tme, w), F32), pltpu.SemaphoreType.DMA((2,))]),
        out_shape=jax.ShapeDtypeStruct((n_tiles * tme, w), F32),
        compiler_params=_cparams(("arbitrary",)),
        name="moe_dispatch",
    )(zero_flag, dest.reshape(n // td, 1, td), rows)


def _moe_sorted_kernel(blk_ref, ea_ref, eb_ref, valid_ref, ts_ref, w1a, w3a, w2a, w1b, w3b, w2b, o_ref):
    j = pl.program_id(0)
    d = o_ref.shape[1]

    @pl.when(valid_ref[j] == 1)
    def _():
        t = ts_ref[:, 0:d].astype(BF16)

        def expert(w1, w3, w2):
            h1 = _dot(t, w1[0])
            he = (h1 * _sigmoid(h1)) * _dot(t, w3[0])
            return _dot(he.astype(BF16), w2[0])

        o_ref[...] = ts_ref[:, d:d + 1] * expert(w1a, w3a, w2a) + ts_ref[:, d + 1:d + 2] * expert(w1b, w3b, w2b)

    @pl.when(valid_ref[j] == 0)
    def _():
        o_ref[...] = jnp.zeros_like(o_ref)


def _moe_sorted(rows_sorted, blk, ea, eb, valid, w1, w3, w2, tme):
    p, w = rows_sorted.shape
    ne, d, de = w1.shape
    n_tiles = p // tme
    up = lambda sel: pl.BlockSpec((1, d, de), lambda j, blk, ea, eb, valid: ((ea, eb)[sel][j], 0, 0))
    down = lambda sel: pl.BlockSpec((1, de, d), lambda j, blk, ea, eb, valid: ((ea, eb)[sel][j], 0, 0))
    return pl.pallas_call(
        _moe_sorted_kernel,
        grid_spec=pltpu.PrefetchScalarGridSpec(
            num_scalar_prefetch=4,
            grid=(n_tiles,),
            in_specs=[pl.BlockSpec((tme, w), lambda j, blk, ea, eb, valid: (blk[j], 0)),
                      up(0), up(0), down(0), up(1), up(1), down(1)],
            out_specs=pl.BlockSpec((tme, d), lambda j, blk, ea, eb, valid: (j, 0))),
        out_shape=jax.ShapeDtypeStruct((p, d), F32),
        compiler_params=_cparams(("arbitrary",)),
        name="moe_sorted",
    )(blk, ea, eb, valid, rows_sorted, w1, w3, w2, w1, w3, w2)


def _final_kernel(dcur_ref, dnxt_ref, xp_ref, mod_ref, modf_ref, gf_ref, ys_ref, y_ref, gbuf, sem, *, tm):
    step = pl.program_id(0) * pl.num_programs(1) + pl.program_id(1)
    n_steps = pl.num_programs(0) * pl.num_programs(1)
    slot = step % 2

    def row_copy(dref, c, k, s):
        r0 = pl.multiple_of(c * SUBLANES, SUBLANES)
        return pltpu.make_async_copy(ys_ref.at[pl.ds(dref[0, 0, r0 + k], 1)], gbuf.at[s, pl.ds(r0 + k, 1)],
                                     sem.at[s])

    def issue(dref, s):
        def body(c, carry):
            for k in range(SUBLANES):
                row_copy(dref, c, k, s).start()
            return carry
        lax.fori_loop(0, tm // SUBLANES, body, 0)

    @pl.when(step == 0)
    def _():
        issue(dcur_ref, 0)

    @pl.when(step + 1 < n_steps)
    def _():
        issue(dnxt_ref, 1 - slot)

    def wait_rows(c, carry):
        for k in range(SUBLANES):
            row_copy(dcur_ref, c, k, slot).wait()
        return carry

    lax.fori_loop(0, tm // SUBLANES, wait_rows, 0)
    xp2 = xp_ref[0] + mod_ref[0][5] * gbuf[slot]
    mf = modf_ref[0]
    y_ref[0] = _rms(xp2, gf_ref[...]) * (1.0 + mf[1]) + mf[0]


def _final(ys, dest, xp, mod, modf, gf, tm=512):
    b, s, d = xp.shape
    tm = min(tm, s)
    n_i = s // tm
    n_blk = b * n_i
    dest3 = dest.reshape(n_blk, 1, tm)
    tile = pl.BlockSpec((1, tm, d), lambda bi, i: (bi, i, 0))
    return pl.pallas_call(
        functools.partial(_final_kernel, tm=tm),
        grid=(b, n_i),
        in_specs=[pl.BlockSpec((1, 1, tm), lambda bi, i: (bi * n_i + i, 0, 0), memory_space=pltpu.SMEM),
                  pl.BlockSpec((1, 1, tm), lambda bi, i: (jnp.minimum(bi * n_i + i + 1, n_blk - 1), 0, 0),
                               memory_space=pltpu.SMEM),
                  tile,
                  pl.BlockSpec((1, N_MOD, 1, d), lambda bi, i: (bi, 0, 0, 0)),
                  pl.BlockSpec((1, 2, 1, d), lambda bi, i: (bi, 0, 0, 0)),
                  pl.BlockSpec(gf.shape, lambda bi, i: (0, 0)),
                  pl.BlockSpec(memory_space=pl.ANY)],
        out_specs=tile,
        out_shape=jax.ShapeDtypeStruct((b, s, d), F32),
        scratch_shapes=[pltpu.VMEM((2, tm, d), F32), pltpu.SemaphoreType.DMA((2,))],
        compiler_params=_cparams(("arbitrary", "arbitrary")),
        name="moe_combine_final",
    )(dest3, dest3, xp, mod, modf, gf, ys)


def _route_tables(cnt, cls, rank, n_tiles, tme):
    ntile_c = (cnt + tme - 1) // tme
    tile_end = jnp.cumsum(ntile_c)
    offs = (tile_end - ntile_c) * tme
    dest = offs[cls] + rank
    n_valid = tile_end[-1]
    j = jnp.arange(n_tiles, dtype=jnp.int32)
    valid = j < n_valid
    blk = jnp.where(valid, j, n_valid - 1)
    tcls = jnp.clip(jnp.searchsorted(tile_end, blk, side="right"), 0, N_CLASSES - 1).astype(jnp.int32)
    pair_lo = jnp.array([0, 0, 0, 1, 1, 2], jnp.int32)
    pair_hi = jnp.array([1, 2, 3, 2, 3, 3], jnp.int32)
    grp, pair = tcls // N_PAIRS, tcls % N_PAIRS
    ea = grp * EXPERTS_PER_GROUP + pair_lo[pair]
    eb = grp * EXPERTS_PER_GROUP + pair_hi[pair]
    zero_flag = jnp.logical_or(jnp.logical_not(valid), j == tile_end[tcls] - 1)
    return (dest.astype(jnp.int32), blk.astype(jnp.int32), ea, eb, valid.astype(jnp.int32),
            zero_flag.astype(jnp.int32))


def _moe_kernel(t_ref, gate_ref, xp_ref, mod_ref, modf_ref, gf_ref, w1_ref, w3_ref, w2_ref, y_ref, acc):
    e = pl.program_id(2)

    @pl.when(e == 0)
    def _():
        acc[...] = jnp.zeros_like(acc)

    t = t_ref[0]
    h1 = _dot(t, w1_ref[0])
    he = (h1 * _sigmoid(h1)) * _dot(t, w3_ref[0])
    gate = gate_ref[0]
    lane = lax.broadcasted_iota(jnp.int32, gate.shape, 1)
    gcol = jnp.sum(jnp.where(lane == e, gate, 0.0), axis=1, keepdims=True)
    acc[...] += gcol * _dot(he.astype(BF16), w2_ref[0])

    @pl.when(e == pl.num_programs(2) - 1)
    def _():
        xp2 = xp_ref[0] + mod_ref[0][5] * acc[...]
        mf = modf_ref[0]
        y_ref[0] = _rms(xp2, gf_ref[...]) * (1.0 + mf[1]) + mf[0]


def _moe(t, gate, xp, mod, modf, gf, w1, w3, w2, tm=512):
    b, s, d = xp.shape
    tm = min(tm, s)
    ne, _, de = w1.shape
    rmod = mod.shape[2]

    def mod_spec(n):
        return (pl.BlockSpec((1, n, 1, d), lambda bi, i, e: (bi, 0, 0, 0)) if rmod == 1 else
                pl.BlockSpec((1, n, tm, d), lambda bi, i, e: (bi, 0, i, 0)))

    tile = lambda w: pl.BlockSpec((1, tm, w), lambda bi, i, e: (bi, i, 0))
    return pl.pallas_call(
        _moe_kernel,
        grid=(b, s // tm, ne),
        in_specs=[tile(d), tile(ne), tile(d), mod_spec(N_MOD), mod_spec(2),
                  pl.BlockSpec(gf.shape, lambda bi, i, e: (0, 0)),
                  pl.BlockSpec((1, d, de), lambda bi, i, e: (e, 0, 0)),
                  pl.BlockSpec((1, d, de), lambda bi, i, e: (e, 0, 0)),
                  pl.BlockSpec((1, de, d), lambda bi, i, e: (e, 0, 0))],
        out_specs=tile(d),
        out_shape=jax.ShapeDtypeStruct((b, s, d), F32),
        scratch_shapes=[pltpu.VMEM((tm, d), F32)],
        compiler_params=_cparams(("arbitrary", "arbitrary", "arbitrary")),
        name="moe_final",
    )(t, gate, xp, mod, modf, gf, w1, w3, w2)


def _prep_w_in(w_in, b_f):
    a = ATTN_WIDTH
    d = w_in.shape[0]
    cwid = (w_in.shape[1] - 3 * a - N_HEADS) // 2
    wf = w_in[:, 3 * a:3 * a + N_HEADS]
    lane_head = jnp.concatenate([jnp.repeat(jnp.arange(N_HEADS), AUG_PER_HEAD), jnp.arange(N_HEADS)])
    npad = LANES - lane_head.shape[0]
    wf_pad = jnp.concatenate([wf[:, lane_head], jnp.zeros((d, npad), F32)], axis=1)
    bf_pad = jnp.concatenate([b_f[lane_head], jnp.zeros((npad,), F32)]).reshape(1, LANES)
    w_cat = jnp.concatenate([w_in[:, 0:3 * a], w_in[:, 3 * a + N_HEADS:], wf_pad], axis=1).astype(BF16)
    del cwid
    return w_cat, bf_pad


def kernel(x_prompt, x_sample, cache_k, cache_v, cache_logf, state_conv, page_table, c_prompt, c_sample, w_ada, b_ada, g_norm1, g_norm2, w_in, b_f, conv_w, conv_b, conv_ln_g, conv_ln_b, w_out, w_rg, b_rg, w_re, b_re, w1, w3, w2, w_ada_f, b_ada_f, g_final):
    depth = w_ada.shape[0]
    assert depth == 1
    b, s, d = x_prompt.shape
    bd, ds, _ = x_sample.shape
    assert ds == 1
    a = ATTN_WIDTH
    n_phys, page = cache_k.shape[1], cache_k.shape[2]

    c_all = jnp.concatenate([c_prompt, c_sample], axis=0)
    mod = _ada(c_all, w_ada[0], b_ada[0]).reshape(b + bd, N_MOD, d)
    modf = _ada(c_all, w_ada_f, b_ada_f).reshape(b + bd, 2, d)
    mp = mod[:b].reshape(b, N_MOD, 1, d)
    ms = mod[b:].transpose(1, 0, 2).reshape(1, N_MOD, bd, d)
    mfp = modf[:b].reshape(b, 2, 1, d)
    mfs = modf[b:].transpose(1, 0, 2).reshape(1, 2, bd, d)

    w_cat, bf_pad = _prep_w_in(w_in[0], b_f[0])
    row = lambda v: v.reshape(1, -1)
    g1, g2, gf = row(g_norm1[0]), row(g_norm2[0]), row(g_final)
    cw, cb, lg, lb = conv_w[0], row(conv_b[0]), row(conv_ln_g[0]), row(conv_ln_b[0])
    wo = w_out[0].astype(BF16)
    nr = N_GROUPS + N_EXPERTS
    wr_pad = jnp.concatenate([w_rg[0], w_re[0], jnp.zeros((d, LANES - nr), F32)], axis=1)
    br_pad = jnp.concatenate([b_rg[0], b_re[0], jnp.zeros((LANES - nr,), F32)]).reshape(1, LANES)
    w1b, w3b, w2b = w1[0].astype(BF16), w3[0].astype(BF16), w2[0].astype(BF16)

    q, k_p, v_p, lf_p, kb, vb, qa, ka, cv_p, st_p = _inproj(x_prompt, mp, g1, w_cat, bf_pad, conv=(cw, cb, lg, lb))
    a_p = _attn(q, qa, kb, ka, vb)
    xp1, rows, ri, cnt = _outproj(a_p, cv_p, x_prompt, mp, g2, wo, wr_pad, br_pad, sparse=True)
    n = b * s
    tme = min(MOE_TILE, n)
    n_tiles = n // tme + N_CLASSES
    dest, blk, ea, eb, valid, zero_flag = _route_tables(
        cnt[:N_CLASSES, 0], ri[:, 0, :].reshape(n), ri[:, 1, :].reshape(n), n_tiles, tme)
    rows_sorted = _dispatch(rows.reshape(n, d + LANES), dest, zero_flag, n_tiles, tme)
    ys = _moe_sorted(rows_sorted, blk, ea, eb, valid, w1b, w3b, w2b, tme)
    y_prompt = _final(ys, dest, xp1, mp, mfp, gf)

    xs = x_sample.reshape(1, bd, d)
    q_s, k_s, v_s, lf_s, u_s = _inproj(xs, ms, g1, w_cat, bf_pad)
    cv_s, st_s = _sconv(u_s[0], state_conv[0].transpose(1, 0, 2), cw, cb, lg, lb)
    a_s = _decode(q_s.reshape(bd, 1, a), k_s.reshape(bd, 1, a), v_s.reshape(bd, 1, a),
                  lf_s.reshape(bd, 1, N_HEADS),
                  cache_k[0].transpose(0, 2, 3, 1).reshape(n_phys, a, page),
                  cache_v[0].transpose(0, 2, 3, 1).reshape(n_phys, a, page),
                  cache_logf[0].transpose(0, 2, 1), page_table)
    xs1, t_s, gate_s = _outproj(a_s.reshape(1, bd, a), cv_s.reshape(1, bd, -1), xs, ms, g2, wo, wr_pad, br_pad,
                                sparse=False)
    y_sample = _moe(t_s, gate_s, xs1, ms, mfs, gf, w1b, w3b, w2b).reshape(bd, 1, d)

    hs = (N_HEADS, HEAD_DIM)
    return (y_prompt, y_sample,
            k_p.reshape(1, b, s, *hs), v_p.reshape(1, b, s, *hs), lf_p.reshape(1, b, s, N_HEADS), st_p[None],
            k_s.reshape(1, bd, 1, *hs), v_s.reshape(1, bd, 1, *hs), lf_s.reshape(1, bd, 1, N_HEADS),
            st_s.transpose(1, 0, 2)[None])
```

```python
import functools

import jax
import jax.numpy as jnp
from jax import lax
from jax.experimental import pallas as pl
from jax.experimental.pallas import tpu as pltpu

F32 = jnp.float32
BF16 = jnp.bfloat16

EPS = 1e-6
N_HEADS = 8
HEAD_DIM = 64
ATTN_WIDTH = N_HEADS * HEAD_DIM
CONV_K = 31
N_GROUPS = 4
EXPERTS_PER_GROUP = 4
N_EXPERTS = N_GROUPS * EXPERTS_PER_GROUP
N_MOD = 6
LANES = 128
SUBLANES = 8
N_PAIRS = 6
N_CLASSES = N_GROUPS * N_PAIRS
CLASS_ROWS = 32
MOE_TILE = 512
AUG_PER_HEAD = 6
LOGF_LANE0 = AUG_PER_HEAD * N_HEADS
CONV_HALO = 32
NEG_BIG = -1e30
VMEM_LIMIT = 56 * 1024 * 1024


def _dot(a, b):
    return jnp.dot(a, b, preferred_element_type=F32)


def _dot_nt(a, b):
    return lax.dot_general(a, b, (((1,), (1,)), ((), ())), preferred_element_type=F32)


def _split2(x):
    hi = x.astype(BF16)
    lo = (x - hi.astype(F32)).astype(BF16)
    return hi, lo


def _split3(x):
    hi = x.astype(BF16)
    r1 = x - hi.astype(F32)
    mid = r1.astype(BF16)
    lo = (r1 - mid.astype(F32)).astype(BF16)
    return hi, mid, lo


def _dot_hp(a, b):
    ah, al = _split2(a)
    bh, bl = _split2(b)
    return _dot(ah, bh) + (_dot(ah, bl) + _dot(al, bh))


def _sigmoid(x):
    return 1.0 / (1.0 + jnp.exp(-x))


def _log_sigmoid(x):
    return jnp.minimum(x, 0.0) - jnp.log1p(jnp.exp(-jnp.abs(x)))


def _rms(x, g):
    return x * lax.rsqrt(jnp.mean(x * x, axis=-1, keepdims=True) + EPS) * g


def _cparams(sem):
    return pltpu.CompilerParams(dimension_semantics=sem, vmem_limit_bytes=VMEM_LIMIT)


def _ada_kernel(c_ref, w_ref, b_ref, o_ref):
    c = c_ref[...]
    o_ref[...] = _dot_hp(c * _sigmoid(c), w_ref[...]) + b_ref[...]


def _ada(c, w, b, bn=1024):
    r, d = c.shape
    n = w.shape[1]
    return pl.pallas_call(
        _ada_kernel,
        grid=(n // bn,),
        in_specs=[pl.BlockSpec((r, d), lambda j: (0, 0)),
                  pl.BlockSpec((d, bn), lambda j: (0, j)),
                  pl.BlockSpec((1, bn), lambda j: (0, j))],
        out_specs=pl.BlockSpec((r, bn), lambda j: (0, j)),
        out_shape=jax.ShapeDtypeStruct((r, n), F32),
        compiler_params=_cparams(("arbitrary",)),
        name="ada",
    )(c, w, b.reshape(1, n))


def _conv_tail(y, conv_b, ln_g, ln_b):
    yf = y + conv_b
    mu = jnp.mean(yf, axis=-1, keepdims=True)
    d = yf - mu
    yn = d * lax.rsqrt(jnp.mean(d * d, axis=-1, keepdims=True) + EPS) * ln_g + ln_b
    return yn * _sigmoid(yn)


def _inproj_kernel(*refs, tm, n_i, prompt, conv_rows):
    if prompt:
        (x_ref, mod_ref, g_ref, w_ref, bf_ref, cw_ref, cb_ref, lg_ref, lb_ref,
         q_ref, k_ref, v_ref, lf_ref, kb_ref, vb_ref, qa_ref, ka_ref, cv_ref, st_ref,
         ubuf, fcarry) = refs
    else:
        (x_ref, mod_ref, g_ref, w_ref, bf_ref,
         q_ref, k_ref, v_ref, lf_ref, u_ref) = refs
    i = pl.program_id(1)
    a = ATTN_WIDTH
    cwid = (w_ref.shape[1] - 3 * a - LANES) // 2
    o = 3 * a

    if prompt:
        @pl.when(i == 0)
        def _():
            ubuf[0, 0:CONV_HALO, :] = jnp.zeros((CONV_HALO, cwid), F32)
            fcarry[...] = jnp.zeros_like(fcarry)

    x = x_ref[0]
    m = mod_ref[0]
    hb = (_rms(x, g_ref[...]) * (1.0 + m[1]) + m[0]).astype(BF16)

    u = _dot(hb, w_ref[:, o:o + cwid]) * _sigmoid(_dot(hb, w_ref[:, o + cwid:o + 2 * cwid]))
    q_ref[0] = (_dot(hb, w_ref[:, 0:a]) * (HEAD_DIM ** -0.5)).astype(BF16)
    k = _dot(hb, w_ref[:, a:2 * a])
    k_ref[0] = k
    v = _dot(hb, w_ref[:, 2 * a:3 * a])
    v_ref[0] = v
    lf = _log_sigmoid(_dot(hb, w_ref[:, o + 2 * cwid:]) + bf_ref[...])
    lf_ref[0] = lf[:, LOGF_LANE0:LOGF_LANE0 + N_HEADS]
    if not prompt:
        u_ref[0] = u
        return

    kb_ref[0] = k.astype(BF16)
    vb_ref[0] = v.astype(BF16)

    cb = 256 if tm % 256 == 0 else tm
    row = lax.broadcasted_iota(jnp.int32, (cb, cb), 0)
    col = lax.broadcasted_iota(jnp.int32, (cb, cb), 1)
    tri = (row >= col).astype(BF16)
    parts = jnp.concatenate(_split3(lf), axis=1)
    carry = fcarry[...]
    lane = lax.broadcasted_iota(jnp.int32, (1, LANES), 1)
    r = lane % AUG_PER_HEAD
    valid = lane < LOGF_LANE0
    for c in range(tm // cb):
        p3 = _dot(tri, parts[c * cb:(c + 1) * cb, :])
        fc = (p3[:, 0:LANES] + p3[:, LANES:2 * LANES] + p3[:, 2 * LANES:]) + carry
        carry = fc[cb - 1:cb, :]
        fh = fc.astype(BF16).astype(F32)
        r1 = fc - fh
        fm = r1.astype(BF16).astype(F32)
        fl = (r1 - fm).astype(BF16).astype(F32)
        qa = jnp.where(r == 0, fh, jnp.where(r == 1, fm, jnp.where(r == 2, fl, 1.0)))
        ka = jnp.where(r < 3, 1.0, jnp.where(r == 3, -fh, jnp.where(r == 4, -fm, -fl)))
        qa_ref[0, c * cb:(c + 1) * cb, :] = jnp.where(valid, qa, 0.0).astype(BF16)
        ka_ref[0, c * cb:(c + 1) * cb, :] = jnp.where(valid, ka, 0.0).astype(BF16)
    fcarry[...] = carry

    nb = tm + CONV_HALO
    ubuf[0, CONV_HALO:nb, :] = u
    for r in range(1, SUBLANES):
        ubuf[r, SUBLANES:nb, :] = ubuf[0, SUBLANES - r:nb - r, :]
    groups = conv_rows // SUBLANES
    for c in range(tm // conv_rows):
        base = CONV_HALO + c * conv_rows
        acc = None
        for d in range(CONV_K):
            a8, r = (d // SUBLANES) * SUBLANES, d % SUBLANES
            rows = ubuf[r, base - a8:base - a8 + conv_rows, :].reshape(groups, SUBLANES, cwid)
            term = cw_ref[CONV_K - 1 - d][None] * rows
            acc = term if acc is None else acc + term
        cv_ref[0, c * conv_rows:(c + 1) * conv_rows, :] = _conv_tail(
            acc.reshape(conv_rows, cwid), cb_ref[...], lg_ref[...], lb_ref[...]).astype(BF16)

    @pl.when(i == n_i - 1)
    def _():
        st_ref[0] = ubuf[0, nb - (CONV_K - 1):nb, :]

    ubuf[0, 0:CONV_HALO, :] = ubuf[0, tm:nb, :]


def _inproj(x, mod, g, w_cat, bf_pad, conv=None, tm=512):
    b, s, d = x.shape
    tm = min(tm, s)
    n_i = s // tm
    a = ATTN_WIDTH
    cwid = (w_cat.shape[1] - 3 * a - LANES) // 2
    rmod = mod.shape[2]
    mod_spec = (pl.BlockSpec((1, N_MOD, 1, d), lambda bi, i: (bi, 0, 0, 0)) if rmod == 1 else
                pl.BlockSpec((1, N_MOD, tm, d), lambda bi, i: (bi, 0, i, 0)))
    tile = lambda w: pl.BlockSpec((1, tm, w), lambda bi, i: (bi, i, 0))
    full = lambda arr: pl.BlockSpec(arr.shape, lambda bi, i: (0,) * arr.ndim)
    prompt = conv is not None
    ins = [x, mod, g, w_cat, bf_pad]
    in_specs = [tile(d), mod_spec, full(g), full(w_cat), full(bf_pad)]
    outs = [jax.ShapeDtypeStruct((b, s, a), BF16), jax.ShapeDtypeStruct((b, s, a), F32),
            jax.ShapeDtypeStruct((b, s, a), F32), jax.ShapeDtypeStruct((b, s, N_HEADS), F32)]
    out_specs = [tile(a), tile(a), tile(a), tile(N_HEADS)]
    scratch = []
    if prompt:
        ins += list(conv)
        in_specs += [full(c) for c in conv]
        outs += [jax.ShapeDtypeStruct((b, s, a), BF16), jax.ShapeDtypeStruct((b, s, a), BF16),
                 jax.ShapeDtypeStruct((b, s, LANES), BF16), jax.ShapeDtypeStruct((b, s, LANES), BF16),
                 jax.ShapeDtypeStruct((b, s, cwid), BF16), jax.ShapeDtypeStruct((b, CONV_K - 1, cwid), F32)]
        out_specs += [tile(a), tile(a), tile(LANES), tile(LANES), tile(cwid),
                      pl.BlockSpec((1, CONV_K - 1, cwid), lambda bi, i: (bi, 0, 0))]
        scratch = [pltpu.VMEM((SUBLANES, tm + CONV_HALO, cwid), F32), pltpu.VMEM((1, LANES), F32)]
    else:
        outs += [jax.ShapeDtypeStruct((b, s, cwid), F32)]
        out_specs += [tile(cwid)]
    return pl.pallas_call(
        functools.partial(_inproj_kernel, tm=tm, n_i=n_i, prompt=prompt, conv_rows=min(32, tm)),
        grid=(b, n_i),
        in_specs=in_specs,
        out_specs=out_specs,
        out_shape=outs,
        scratch_shapes=scratch,
        compiler_params=_cparams(("arbitrary", "arbitrary")),
        name="inproj_prompt" if prompt else "inproj_sample",
    )(*ins)


def _attn_kernel(q_ref, qa_ref, k_ref, ka_ref, v_ref, o_ref, lhs_scr, *, tq):
    i = pl.program_id(1)
    lane = lax.broadcasted_iota(jnp.int32, (1, LANES), 1)
    lo = lane < HEAD_DIM
    qa = qa_ref[0]
    zero = jnp.zeros((), BF16)
    n_slabs = N_HEADS // 2

    for h in range(N_HEADS):
        q2 = q_ref[0, :, (h // 2) * LANES:(h // 2 + 1) * LANES]
        lhs_scr[h, :, 0:LANES] = jnp.where(lo if h % 2 == 0 else jnp.logical_not(lo), q2, zero)
        lhs_scr[h, :, LANES:] = jnp.where((lane >= AUG_PER_HEAD * h) & (lane < AUG_PER_HEAD * (h + 1)), qa, zero)

    def step(t, carry, masked):
        ms, ls, accs = carry
        ks = pl.multiple_of(t * tq, tq)
        kaug = ka_ref[0, pl.ds(ks, tq), :]
        ones = jnp.ones((tq, LANES), BF16)
        if masked:
            causal = (lax.broadcasted_iota(jnp.int32, (tq, tq), 1) <=
                      lax.broadcasted_iota(jnp.int32, (tq, tq), 0))
        new_m, new_l, new_acc = [], [], []
        for j in range(n_slabs):
            sl = slice(j * LANES, (j + 1) * LANES)
            rhs = jnp.concatenate([k_ref[0, pl.ds(ks, tq), sl], kaug], axis=1)
            v2 = jnp.concatenate([v_ref[0, pl.ds(ks, tq), sl], ones], axis=1)
            alphas, pvs = [], []
            for h in (2 * j, 2 * j + 1):
                s = _dot_nt(lhs_scr[h], rhs)
                if masked:
                    s = jnp.where(causal, s, -jnp.inf)
                m_new = jnp.maximum(ms[h], jnp.max(s, axis=1, keepdims=True))
                alphas.append(jnp.exp(ms[h] - m_new))
                new_m.append(m_new)
                pvs.append(_dot(jnp.exp(s - m_new).astype(BF16), v2))
            alpha2 = jnp.where(lo, alphas[0], alphas[1])
            new_acc.append(accs[j] * alpha2 + jnp.where(lo, pvs[0][:, 0:LANES], pvs[1][:, 0:LANES]))
            new_l.append(ls[j] * alpha2 + jnp.where(lo, pvs[0][:, LANES:], pvs[1][:, LANES:]))
        return tuple(new_m), tuple(new_l), tuple(new_acc)

    init = (tuple(jnp.full((tq, 1), NEG_BIG, F32) for _ in range(N_HEADS)),
            tuple(jnp.zeros((tq, LANES), F32) for _ in range(n_slabs)),
            tuple(jnp.zeros((tq, LANES), F32) for _ in range(n_slabs)))
    carry = lax.fori_loop(0, i, functools.partial(step, masked=False), init)
    _, ls, accs = step(i, carry, True)
    for j in range(n_slabs):
        o_ref[0, :, j * LANES:(j + 1) * LANES] = (accs[j] / ls[j]).astype(BF16)


def _attn(q, qa, kb, ka, vb, tq=512):
    b, s, a = q.shape
    tq = min(tq, s)
    tile = lambda w: pl.BlockSpec((1, tq, w), lambda bi, i: (bi, i, 0))
    whole = lambda w: pl.BlockSpec((1, s, w), lambda bi, i: (bi, 0, 0))
    return pl.pallas_call(
        functools.partial(_attn_kernel, tq=tq),
        grid=(b, s // tq),
        in_specs=[tile(a), tile(LANES), whole(a), whole(LANES), whole(a)],
        out_specs=tile(a),
        out_shape=jax.ShapeDtypeStruct((b, s, a), BF16),
        scratch_shapes=[pltpu.VMEM((N_HEADS, tq, 2 * LANES), BF16)],
        compiler_params=_cparams(("arbitrary", "arbitrary")),
        name="attn_prompt",
    )(q, qa, kb, ka, vb)


def _sconv_kernel(u_ref, st_ref, cw_ref, cb_ref, lg_ref, lb_ref, cv_ref, nst_ref):
    u = u_ref[...]
    y = u * cw_ref[CONV_K - 1:CONV_K, :]
    for j in range(CONV_K - 1):
        y = y + st_ref[j] * cw_ref[j:j + 1, :]
    cv_ref[...] = _conv_tail(y, cb_ref[...], lg_ref[...], lb_ref[...]).astype(BF16)
    for j in range(CONV_K - 2):
        nst_ref[j] = st_ref[j + 1]
    nst_ref[CONV_K - 2] = u


def _sconv(u, state, cw, cb, lg, lb):
    bd, c = u.shape
    return pl.pallas_call(
        _sconv_kernel,
        out_shape=[jax.ShapeDtypeStruct((bd, c), BF16), jax.ShapeDtypeStruct(state.shape, F32)],
        compiler_params=pltpu.CompilerParams(vmem_limit_bytes=VMEM_LIMIT),
        name="conv_sample",
    )(u, state, cw, cb, lg, lb)


def _decode_kernel(pt_ref, q_ref, kn_ref, vn_ref, lfn_ref, *refs, pp, n_steps):
    kt_refs = refs[0:pp]
    vt_refs = refs[pp:2 * pp]
    lf_refs = refs[2 * pp:3 * pp]
    o_ref = refs[3 * pp]
    m_scr, l_scr, acc_scr, f_scr = refs[3 * pp + 1:]
    st = pl.program_id(1)
    a = ATTN_WIDTH
    page = kt_refs[0].shape[2]

    @pl.when(st == 0)
    def _():
        m_scr[...] = jnp.full_like(m_scr, NEG_BIG)
        l_scr[...] = jnp.zeros_like(l_scr)
        acc_scr[...] = jnp.zeros_like(acc_scr)
        f_scr[...] = jnp.zeros_like(f_scr)

    head_of_lane = lax.broadcasted_iota(jnp.int32, (N_HEADS, a), 1) // HEAD_DIM
    head_row = lax.broadcasted_iota(jnp.int32, (N_HEADS, a), 0)
    diag = head_of_lane == head_row
    qbd = jnp.where(diag, jnp.broadcast_to(q_ref[0].astype(F32), (N_HEADS, a)), 0.0)
    qbd_b = qbd.astype(BF16)

    rowi = lax.broadcasted_iota(jnp.int32, (page, page), 0)
    coli = lax.broadcasted_iota(jnp.int32, (page, page), 1)
    triu = (rowi <= coli).astype(BF16)
    fc = f_scr[...]
    s_parts = []
    for r in range(pp):
        lf = lf_refs[r][0]
        p3 = _dot(jnp.concatenate(_split3(lf), axis=0), triu)
        fpre = (p3[0:N_HEADS] + p3[N_HEADS:2 * N_HEADS] + p3[2 * N_HEADS:]) + fc
        fc = fc + jnp.sum(lf, axis=1, keepdims=True)
        s_parts.append(_dot(qbd_b, kt_refs[r][0].astype(BF16)) - fpre)
    f_scr[...] = fc
    s = jnp.concatenate(s_parts, axis=1)
    m_old = m_scr[...]
    m_new = jnp.maximum(m_old, jnp.max(s, axis=1, keepdims=True))
    alpha = jnp.exp(m_old - m_new)
    p = jnp.exp(s - m_new)
    l_new = alpha * l_scr[...] + jnp.sum(p, axis=1, keepdims=True)
    pb = p.astype(BF16)
    acc = acc_scr[...] * alpha
    for r in range(pp):
        acc = acc + _dot_nt(pb[:, r * page:(r + 1) * page], vt_refs[r][0].astype(BF16))
    m_scr[...] = m_new
    l_scr[...] = l_new
    acc_scr[...] = acc

    @pl.when(st == n_steps - 1)
    def _():
        eye8 = (lax.broadcasted_iota(jnp.int32, (N_HEADS, N_HEADS), 0) ==
                lax.broadcasted_iota(jnp.int32, (N_HEADS, N_HEADS), 1))
        f_new = fc + jnp.sum(jnp.where(eye8, jnp.broadcast_to(lfn_ref[0], (N_HEADS, N_HEADS)), 0.0),
                             axis=1, keepdims=True)
        s_self = jnp.sum(qbd * kn_ref[0], axis=1, keepdims=True) - f_new
        m_f = jnp.maximum(m_new, s_self)
        al = jnp.exp(m_new - m_f)
        p_self = jnp.exp(s_self - m_f)
        l_f = l_new * al + p_self
        acc_f = acc * al + p_self * vn_ref[0]
        o = jnp.where(diag, acc_f / l_f, 0.0)
        o_ref[0] = jnp.sum(o, axis=0, keepdims=True).astype(BF16)


def _decode(q, k_new, v_new, logf_new, cache_k, cache_v, cache_logf, page_table, pp=16):
    bd, n_pages = page_table.shape
    a = ATTN_WIDTH
    page = cache_k.shape[2]
    pp = min(pp, n_pages)
    n_steps = n_pages // pp
    row = lambda w: pl.BlockSpec((1, 1, w), lambda b, s, pt: (b, 0, 0))

    def paged(w, r):
        return pl.BlockSpec((1, w, page), lambda b, s, pt: (pt[b * n_pages + s * pp + r], 0, 0))

    in_specs = ([row(a), row(a), row(a), row(N_HEADS)] +
                [paged(a, r) for r in range(pp)] + [paged(a, r) for r in range(pp)] +
                [paged(N_HEADS, r) for r in range(pp)])
    return pl.pallas_call(
        functools.partial(_decode_kernel, pp=pp, n_steps=n_steps),
        grid_spec=pltpu.PrefetchScalarGridSpec(
            num_scalar_prefetch=1,
            grid=(bd, n_steps),
            in_specs=in_specs,
            out_specs=row(a),
            scratch_shapes=[pltpu.VMEM((N_HEADS, 1), F32), pltpu.VMEM((N_HEADS, 1), F32),
                            pltpu.VMEM((N_HEADS, a), F32), pltpu.VMEM((N_HEADS, 1), F32)]),
        out_shape=jax.ShapeDtypeStruct((bd, 1, a), BF16),
        compiler_params=_cparams(("arbitrary", "arbitrary")),
        name="decode_attn",
    )(page_table.reshape(-1), q, k_new, v_new, logf_new,
      *([cache_k] * pp), *([cache_v] * pp), *([cache_logf] * pp))


def _outproj_kernel(a_ref, cv_ref, x_ref, mod_ref, g_ref, wo_ref, wr_ref, br_ref, *refs, tm, sparse):
    if sparse:
        xp_ref, t_ref, ri_ref, cnt_ref, gt_scr, cnt_scr = refs
    else:
        xp_ref, t_ref, gate_ref, gt_scr = refs
    aw = a_ref.shape[2]
    d = x_ref.shape[2]
    m = mod_ref[0]
    y = _dot(a_ref[0], wo_ref[0:aw, :]) + _dot(cv_ref[0], wo_ref[aw:, :])
    xp = x_ref[0] + m[2] * y
    xp_ref[0] = xp
    t = _rms(xp, g_ref[...]) * (1.0 + m[4]) + m[3]
    if sparse:
        t_ref[0, :, 0:d] = t
    else:
        t_ref[0] = t.astype(BF16)

    th, tl = _split2(t)
    r2 = _dot(th, wr_ref[...])
    lg = (r2[:, 0:LANES] + r2[:, LANES:]) + _dot(tl, wr_ref[:, 0:LANES]) + br_ref[...]
    lt = lg.T
    g = [lt[r:r + 1, :] for r in range(N_GROUPS)]
    gmax = jnp.maximum(jnp.maximum(g[0], g[1]), jnp.maximum(g[2], g[3]))
    gidx = jnp.where(g[0] >= gmax, 0, jnp.where(g[1] >= gmax, 1, jnp.where(g[2] >= gmax, 2, 3)))
    gden = (jnp.exp(g[0] - gmax) + jnp.exp(g[1] - gmax)) + (jnp.exp(g[2] - gmax) + jnp.exp(g[3] - gmax))
    gw = 1.0 / gden
    e = []
    for r in range(EXPERTS_PER_GROUP):
        er = jnp.zeros_like(gmax)
        for gi in range(N_GROUPS):
            c = N_GROUPS + gi * EXPERTS_PER_GROUP + r
            er = jnp.where(gidx == gi, lt[c:c + 1, :], er)
        e.append(er)
    v1 = jnp.maximum(jnp.maximum(e[0], e[1]), jnp.maximum(e[2], e[3]))
    i1 = jnp.where(e[0] >= v1, 0, jnp.where(e[1] >= v1, 1, jnp.where(e[2] >= v1, 2, 3)))
    rest = [jnp.where(i1 == r, -jnp.inf, e[r]) for r in range(EXPERTS_PER_GROUP)]
    v2 = jnp.maximum(jnp.maximum(rest[0], rest[1]), jnp.maximum(rest[2], rest[3]))
    i2 = jnp.where(rest[0] >= v2, 0, jnp.where(rest[1] >= v2, 1, jnp.where(rest[2] >= v2, 2, 3)))
    ex = jnp.exp(v2 - v1)
    w1 = gw / (1.0 + ex)
    w2 = gw * ex / (1.0 + ex)
    gt_scr[...] = jnp.zeros_like(gt_scr)
    if not sparse:
        id1 = gidx * EXPERTS_PER_GROUP + i1
        id2 = gidx * EXPERTS_PER_GROUP + i2
        for ei in range(N_EXPERTS):
            gt_scr[ei:ei + 1, :] = jnp.where(id1 == ei, w1, 0.0) + jnp.where(id2 == ei, w2, 0.0)
        gate_ref[0] = gt_scr[...].T[:, 0:N_EXPERTS]
        return

    first = i1 < i2
    lo = jnp.minimum(i1, i2)
    hi = jnp.maximum(i1, i2)
    pair = jnp.where(lo == 0, hi - 1, jnp.where(lo == 1, hi + 1, N_PAIRS - 1))
    cls = gidx * N_PAIRS + pair
    gt_scr[0:1, :] = jnp.where(first, w1, w2)
    gt_scr[1:2, :] = jnp.where(first, w2, w1)
    t_ref[0, :, d:] = gt_scr[...].T

    @pl.when((pl.program_id(0) == 0) & (pl.program_id(1) == 0))
    def _():
        cnt_scr[...] = jnp.zeros_like(cnt_scr)

    oh = (lax.broadcasted_iota(jnp.int32, (CLASS_ROWS, tm), 0) == cls).astype(F32)
    before = (lax.broadcasted_iota(jnp.int32, (tm, tm), 0) <
              lax.broadcasted_iota(jnp.int32, (tm, tm), 1)).astype(BF16)
    carry = cnt_scr[...]
    pref = _dot(oh.astype(BF16), before) + carry[:, 0:1]
    rank = jnp.sum(oh * pref, axis=0, keepdims=True)
    ri_ref[0] = jnp.zeros(ri_ref.shape[1:], jnp.int32)
    ri_ref[0, 0:1, :] = cls
    ri_ref[0, 1:2, :] = rank.astype(jnp.int32)
    total = carry + jnp.sum(oh, axis=1, keepdims=True)
    cnt_scr[...] = total
    cnt_ref[...] = total.astype(jnp.int32)


def _outproj(a, cv, x, mod, g, wo, wr_pad, br_pad, sparse, tm=512):
    b, s, d = x.shape
    tm = min(tm, s)
    aw, cw = a.shape[2], cv.shape[2]
    rmod = mod.shape[2]
    mod_spec = (pl.BlockSpec((1, N_MOD, 1, d), lambda bi, i: (bi, 0, 0, 0)) if rmod == 1 else
                pl.BlockSpec((1, N_MOD, tm, d), lambda bi, i: (bi, 0, i, 0)))
    tile = lambda w: pl.BlockSpec((1, tm, w), lambda bi, i: (bi, i, 0))
    full = lambda arr: pl.BlockSpec(arr.shape, lambda bi, i: (0,) * arr.ndim)
    if sparse:
        out_specs = [tile(d), tile(d + LANES), pl.BlockSpec((1, SUBLANES, tm), lambda bi, i: (bi, 0, i)),
                     pl.BlockSpec((CLASS_ROWS, LANES), lambda bi, i: (0, 0))]
        out_shape = [jax.ShapeDtypeStruct((b, s, d), F32), jax.ShapeDtypeStruct((b, s, d + LANES), F32),
                     jax.ShapeDtypeStruct((b, SUBLANES, s), jnp.int32),
                     jax.ShapeDtypeStruct((CLASS_ROWS, LANES), jnp.int32)]
        scratch = [pltpu.VMEM((LANES, tm), F32), pltpu.VMEM((CLASS_ROWS, LANES), F32)]
    else:
        out_specs = [tile(d), tile(d), tile(N_EXPERTS)]
        out_shape = [jax.ShapeDtypeStruct((b, s, d), F32), jax.ShapeDtypeStruct((b, s, d), BF16),
                     jax.ShapeDtypeStruct((b, s, N_EXPERTS), F32)]
        scratch = [pltpu.VMEM((LANES, tm), F32)]
    return pl.pallas_call(
        functools.partial(_outproj_kernel, tm=tm, sparse=sparse),
        grid=(b, s // tm),
        in_specs=[tile(aw), tile(cw), tile(d), mod_spec, full(g), full(wo), full(wr_pad), full(br_pad)],
        out_specs=out_specs,
        out_shape=out_shape,
        scratch_shapes=scratch,
        compiler_params=_cparams(("arbitrary", "arbitrary")),
        name="outproj_router_sparse" if sparse else "outproj_router_dense",
    )(a, cv, x, mod, g, wo, wr_pad, br_pad)


def _dispatch_kernel(zf_ref, dest_ref, src_ref, out_ref, zbuf, sem, *, td, tme, n_tiles):
    i = pl.program_id(0)

    def zero_copy(j):
        return pltpu.make_async_copy(zbuf, out_ref.at[pl.ds(pl.multiple_of(j * tme, tme), tme)], sem.at[0])

    @pl.when(i == 0)
    def _():
        zbuf[...] = jnp.zeros_like(zbuf)

        def start(j, c):
            @pl.when(zf_ref[j] == 1)
            def _():
                zero_copy(j).start()
            return c

        def wait(j, c):
            @pl.when(zf_ref[j] == 1)
            def _():
                zero_copy(j).wait()
            return c

        lax.fori_loop(0, n_tiles, start, 0)
        lax.fori_loop(0, n_tiles, wait, 0)

    def row_copy(c, k):
        return pltpu.make_async_copy(src_ref.at[c, pl.ds(k, 1)],
                                     out_ref.at[pl.ds(dest_ref[0, 0, c * SUBLANES + k], 1)], sem.at[1])

    def start_rows(c, carry):
        for k in range(SUBLANES):
            row_copy(c, k).start(priority=k % 2)
        return carry

    def wait_rows(c, carry):
        for k in range(SUBLANES):
            row_copy(c, k).wait()
        return carry

    lax.fori_loop(0, td // SUBLANES, start_rows, 0)
    lax.fori_loop(0, td // SUBLANES, wait_rows, 0)


def _dispatch(rows, dest, zero_flag, n_tiles, tme, td=512):
    n, w = rows.shape
    td = min(td, n)
    return pl.pallas_call(
        functools.partial(_dispatch_kernel, td=td, tme=tme, n_tiles=n_tiles),
        grid_spec=pltpu.PrefetchScalarGridSpec(
            num_scalar_prefetch=1,
            grid=(n // td,),
            in_specs=[pl.BlockSpec((1, 1, td), lambda i, zf: (i, 0, 0), memory_space=pltpu.SMEM),
                      pl.BlockSpec((td // SUBLANES, SUBLANES, w), lambda i, zf: (i, 0, 0))],
            out_specs=pl.BlockSpec(memory_space=pl.ANY),
            scratch_shapes=[pltpu.VMEM((tme, w), F32), pltpu.SemaphoreType.DMA((2,))]),
        out_shape=jax.ShapeDtypeStruct((n_tiles * tme, w), F32),
        compiler_params=_cparams(("arbitrary",)),
        name="moe_dispatch",
    )(zero_flag, dest.reshape(n // td, 1, td), rows.reshape(n // SUBLANES, SUBLANES, w))


def _moe_sorted_kernel(blk_ref, ea_ref, eb_ref, valid_ref, ts_ref, w1a, w3a, w2a, w1b, w3b, w2b, o_ref):
    j = pl.program_id(0)
    d = o_ref.shape[1]

    @pl.when(valid_ref[j] == 1)
    def _():
        t = ts_ref[:, 0:d].astype(BF16)

        def expert(w1, w3, w2):
            h1 = _dot(t, w1[0])
            he = (h1 * _sigmoid(h1)) * _dot(t, w3[0])
            return _dot(he.astype(BF16), w2[0])

        o_ref[...] = ts_ref[:, d:d + 1] * expert(w1a, w3a, w2a) + ts_ref[:, d + 1:d + 2] * expert(w1b, w3b, w2b)

    @pl.when(valid_ref[j] == 0)
    def _():
        o_ref[...] = jnp.zeros_like(o_ref)


def _moe_sorted(rows_sorted, blk, ea, eb, valid, w1, w3, w2, tme):
    p, w = rows_sorted.shape
    ne, d, de = w1.shape
    n_tiles = p // tme
    up = lambda sel: pl.BlockSpec((1, d, de), lambda j, blk, ea, eb, valid: ((ea, eb)[sel][j], 0, 0))
    down = lambda sel: pl.BlockSpec((1, de, d), lambda j, blk, ea, eb, valid: ((ea, eb)[sel][j], 0, 0))
    return pl.pallas_call(
        _moe_sorted_kernel,
        grid_spec=pltpu.PrefetchScalarGridSpec(
            num_scalar_prefetch=4,
            grid=(n_tiles,),
            in_specs=[pl.BlockSpec((tme, w), lambda j, blk, ea, eb, valid: (blk[j], 0)),
                      up(0), up(0), down(0), up(1), up(1), down(1)],
            out_specs=pl.BlockSpec((tme, d), lambda j, blk, ea, eb, valid: (j, 0))),
        out_shape=jax.ShapeDtypeStruct((p, d), F32),
        compiler_params=_cparams(("arbitrary",)),
        name="moe_sorted",
    )(blk, ea, eb, valid, rows_sorted, w1, w3, w2, w1, w3, w2)


def _final_kernel(dcur_ref, dnxt_ref, xp_ref, mod_ref, modf_ref, gf_ref, ys_ref, y_ref, gbuf, sem, *, tm):
    step = pl.program_id(0) * pl.num_programs(1) + pl.program_id(1)
    n_steps = pl.num_programs(0) * pl.num_programs(1)
    slot = step % 2

    def row_copy(dref, c, k, s):
        return pltpu.make_async_copy(ys_ref.at[pl.ds(dref[0, 0, c * SUBLANES + k], 1)],
                                     gbuf.at[s, c, pl.ds(k, 1)], sem.at[s])

    def issue(dref, s):
        def body(c, carry):
            for k in range(SUBLANES):
                row_copy(dref, c, k, s).start(priority=k % 2)
            return carry
        lax.fori_loop(0, tm // SUBLANES, body, 0)

    @pl.when(step == 0)
    def _():
        issue(dcur_ref, 0)

    @pl.when(step + 1 < n_steps)
    def _():
        issue(dnxt_ref, 1 - slot)

    def wait_rows(c, carry):
        for k in range(SUBLANES):
            row_copy(dcur_ref, c, k, slot).wait()
        return carry

    lax.fori_loop(0, tm // SUBLANES, wait_rows, 0)
    xp2 = xp_ref[0] + mod_ref[0][5] * gbuf[slot].reshape(tm, xp_ref.shape[2])
    mf = modf_ref[0]
    y_ref[0] = _rms(xp2, gf_ref[...]) * (1.0 + mf[1]) + mf[0]


def _final(ys, dest, xp, mod, modf, gf, tm=512):
    b, s, d = xp.shape
    tm = min(tm, s)
    n_i = s // tm
    n_blk = b * n_i
    dest3 = dest.reshape(n_blk, 1, tm)
    tile = pl.BlockSpec((1, tm, d), lambda bi, i: (bi, i, 0))
    return pl.pallas_call(
        functools.partial(_final_kernel, tm=tm),
        grid=(b, n_i),
        in_specs=[pl.BlockSpec((1, 1, tm), lambda bi, i: (bi * n_i + i, 0, 0), memory_space=pltpu.SMEM),
                  pl.BlockSpec((1, 1, tm), lambda bi, i: (jnp.minimum(bi * n_i + i + 1, n_blk - 1), 0, 0),
                               memory_space=pltpu.SMEM),
                  tile,
                  pl.BlockSpec((1, N_MOD, 1, d), lambda bi, i: (bi, 0, 0, 0)),
                  pl.BlockSpec((1, 2, 1, d), lambda bi, i: (bi, 0, 0, 0)),
                  pl.BlockSpec(gf.shape, lambda bi, i: (0, 0)),
                  pl.BlockSpec(memory_space=pl.ANY)],
        out_specs=tile,
        out_shape=jax.ShapeDtypeStruct((b, s, d), F32),
        scratch_shapes=[pltpu.VMEM((2, tm // SUBLANES, SUBLANES, d), F32), pltpu.SemaphoreType.DMA((2,))],
        compiler_params=_cparams(("arbitrary", "arbitrary")),
        name="moe_combine_final",
    )(dest3, dest3, xp, mod, modf, gf, ys)


def _route_tables(cnt, cls, rank, n_tiles, tme):
    ntile_c = (cnt + tme - 1) // tme
    tile_end = jnp.cumsum(ntile_c)
    offs = (tile_end - ntile_c) * tme
    dest = offs[cls] + rank
    n_valid = tile_end[-1]
    j = jnp.arange(n_tiles, dtype=jnp.int32)
    valid = j < n_valid
    blk = jnp.where(valid, j, n_valid - 1)
    tcls = jnp.sum((tile_end[None, :] <= blk[:, None]).astype(jnp.int32), axis=1)
    tcls = jnp.minimum(tcls, N_CLASSES - 1)
    pair_lo = jnp.array([0, 0, 0, 1, 1, 2], jnp.int32)
    pair_hi = jnp.array([1, 2, 3, 2, 3, 3], jnp.int32)
    grp, pair = tcls // N_PAIRS, tcls % N_PAIRS
    ea = grp * EXPERTS_PER_GROUP + pair_lo[pair]
    eb = grp * EXPERTS_PER_GROUP + pair_hi[pair]
    zero_flag = jnp.logical_or(jnp.logical_not(valid), j == tile_end[tcls] - 1)
    return (dest.astype(jnp.int32), blk.astype(jnp.int32), ea, eb, valid.astype(jnp.int32),
            zero_flag.astype(jnp.int32))


def _moe_kernel(t_ref, gate_ref, xp_ref, mod_ref, modf_ref, gf_ref, w1_ref, w3_ref, w2_ref, y_ref, acc):
    e = pl.program_id(2)

    @pl.when(e == 0)
    def _():
        acc[...] = jnp.zeros_like(acc)

    t = t_ref[0]
    h1 = _dot(t, w1_ref[0])
    he = (h1 * _sigmoid(h1)) * _dot(t, w3_ref[0])
    gate = gate_ref[0]
    lane = lax.broadcasted_iota(jnp.int32, gate.shape, 1)
    gcol = jnp.sum(jnp.where(lane == e, gate, 0.0), axis=1, keepdims=True)
    acc[...] += gcol * _dot(he.astype(BF16), w2_ref[0])

    @pl.when(e == pl.num_programs(2) - 1)
    def _():
        xp2 = xp_ref[0] + mod_ref[0][5] * acc[...]
        mf = modf_ref[0]
        y_ref[0] = _rms(xp2, gf_ref[...]) * (1.0 + mf[1]) + mf[0]


def _moe(t, gate, xp, mod, modf, gf, w1, w3, w2, tm=512):
    b, s, d = xp.shape
    tm = min(tm, s)
    ne, _, de = w1.shape
    rmod = mod.shape[2]

    def mod_spec(n):
        return (pl.BlockSpec((1, n, 1, d), lambda bi, i, e: (bi, 0, 0, 0)) if rmod == 1 else
                pl.BlockSpec((1, n, tm, d), lambda bi, i, e: (bi, 0, i, 0)))

    tile = lambda w: pl.BlockSpec((1, tm, w), lambda bi, i, e: (bi, i, 0))
    return pl.pallas_call(
        _moe_kernel,
        grid=(b, s // tm, ne),
        in_specs=[tile(d), tile(ne), tile(d), mod_spec(N_MOD), mod_spec(2),
                  pl.BlockSpec(gf.shape, lambda bi, i, e: (0, 0)),
                  pl.BlockSpec((1, d, de), lambda bi, i, e: (e, 0, 0)),
                  pl.BlockSpec((1, d, de), lambda bi, i, e: (e, 0, 0)),
                  pl.BlockSpec((1, de, d), lambda bi, i, e: (e, 0, 0))],
        out_specs=tile(d),
        out_shape=jax.ShapeDtypeStruct((b, s, d), F32),
        scratch_shapes=[pltpu.VMEM((tm, d), F32)],
        compiler_params=_cparams(("arbitrary", "arbitrary", "arbitrary")),
        name="moe_final",
    )(t, gate, xp, mod, modf, gf, w1, w3, w2)


def _prep_w_in(w_in, b_f):
    a = ATTN_WIDTH
    d = w_in.shape[0]
    cwid = (w_in.shape[1] - 3 * a - N_HEADS) // 2
    wf = w_in[:, 3 * a:3 * a + N_HEADS]
    lane_head = jnp.concatenate([jnp.repeat(jnp.arange(N_HEADS), AUG_PER_HEAD), jnp.arange(N_HEADS)])
    npad = LANES - lane_head.shape[0]
    wf_pad = jnp.concatenate([wf[:, lane_head], jnp.zeros((d, npad), F32)], axis=1)
    bf_pad = jnp.concatenate([b_f[lane_head], jnp.zeros((npad,), F32)]).reshape(1, LANES)
    w_cat = jnp.concatenate([w_in[:, 0:3 * a], w_in[:, 3 * a + N_HEADS:], wf_pad], axis=1).astype(BF16)
    del cwid
    return w_cat, bf_pad


def kernel(x_prompt, x_sample, cache_k, cache_v, cache_logf, state_conv, page_table, c_prompt, c_sample, w_ada, b_ada, g_norm1, g_norm2, w_in, b_f, conv_w, conv_b, conv_ln_g, conv_ln_b, w_out, w_rg, b_rg, w_re, b_re, w1, w3, w2, w_ada_f, b_ada_f, g_final):
    depth = w_ada.shape[0]
    assert depth == 1
    b, s, d = x_prompt.shape
    bd, ds, _ = x_sample.shape
    assert ds == 1
    a = ATTN_WIDTH
    n_phys, page = cache_k.shape[1], cache_k.shape[2]

    c_all = jnp.concatenate([c_prompt, c_sample], axis=0)
    mod = _ada(c_all, w_ada[0], b_ada[0]).reshape(b + bd, N_MOD, d)
    modf = _ada(c_all, w_ada_f, b_ada_f).reshape(b + bd, 2, d)
    mp = mod[:b].reshape(b, N_MOD, 1, d)
    ms = mod[b:].transpose(1, 0, 2).reshape(1, N_MOD, bd, d)
    mfp = modf[:b].reshape(b, 2, 1, d)
    mfs = modf[b:].transpose(1, 0, 2).reshape(1, 2, bd, d)

    w_cat, bf_pad = _prep_w_in(w_in[0], b_f[0])
    row = lambda v: v.reshape(1, -1)
    g1, g2, gf = row(g_norm1[0]), row(g_norm2[0]), row(g_final)
    cw, cb, lg, lb = conv_w[0], row(conv_b[0]), row(conv_ln_g[0]), row(conv_ln_b[0])
    wo = w_out[0].astype(BF16)
    nr = N_GROUPS + N_EXPERTS
    wr_f32 = jnp.concatenate([w_rg[0], w_re[0], jnp.zeros((d, LANES - nr), F32)], axis=1)
    wr_pad = jnp.concatenate(_split2(wr_f32), axis=1)
    br_pad = jnp.concatenate([b_rg[0], b_re[0], jnp.zeros((LANES - nr,), F32)]).reshape(1, LANES)
    w1b, w3b, w2b = w1[0].astype(BF16), w3[0].astype(BF16), w2[0].astype(BF16)

    cw_rep = jnp.broadcast_to(cw[:, None, :], (CONV_K, SUBLANES, cw.shape[1]))
    q, k_p, v_p, lf_p, kb, vb, qa, ka, cv_p, st_p = _inproj(x_prompt, mp, g1, w_cat, bf_pad,
                                                           conv=(cw_rep, cb, lg, lb))
    a_p = _attn(q, qa, kb, ka, vb)
    xp1, rows, ri, cnt = _outproj(a_p, cv_p, x_prompt, mp, g2, wo, wr_pad, br_pad, sparse=True)
    n = b * s
    tme = min(MOE_TILE, n)
    n_tiles = n // tme + N_CLASSES
    dest, blk, ea, eb, valid, zero_flag = _route_tables(
        cnt[:N_CLASSES, 0], ri[:, 0, :].reshape(n), ri[:, 1, :].reshape(n), n_tiles, tme)
    rows_sorted = _dispatch(rows.reshape(n, d + LANES), dest, zero_flag, n_tiles, tme)
    ys = _moe_sorted(rows_sorted, blk, ea, eb, valid, w1b, w3b, w2b, tme)
    y_prompt = _final(ys, dest, xp1, mp, mfp, gf)

    xs = x_sample.reshape(1, bd, d)
    q_s, k_s, v_s, lf_s, u_s = _inproj(xs, ms, g1, w_cat, bf_pad)
    cv_s, st_s = _sconv(u_s[0], state_conv[0].transpose(1, 0, 2), cw, cb, lg, lb)
    a_s = _decode(q_s.reshape(bd, 1, a), k_s.reshape(bd, 1, a), v_s.reshape(bd, 1, a),
                  lf_s.reshape(bd, 1, N_HEADS),
                  cache_k[0].transpose(0, 2, 3, 1).reshape(n_phys, a, page),
                  cache_v[0].transpose(0, 2, 3, 1).reshape(n_phys, a, page),
                  cache_logf[0].transpose(0, 2, 1), page_table)
    xs1, t_s, gate_s = _outproj(a_s.reshape(1, bd, a), cv_s.reshape(1, bd, -1), xs, ms, g2, wo, wr_pad, br_pad,
                                sparse=False)
    y_sample = _moe(t_s, gate_s, xs1, ms, mfs, gf, w1b, w3b, w2b).reshape(bd, 1, d)

    hs = (N_HEADS, HEAD_DIM)
    return (y_prompt, y_sample,
            k_p.reshape(1, b, s, *hs), v_p.reshape(1, b, s, *hs), lf_p.reshape(1, b, s, N_HEADS), st_p[None],
            k_s.reshape(1, bd, 1, *hs), v_s.reshape(1, bd, 1, *hs), lf_s.reshape(1, bd, 1, N_HEADS),
            st_s.transpose(1, 0, 2)[None])
```

```python
import functools

import jax
import jax.numpy as jnp
from jax import lax
from jax.experimental import pallas as pl
from jax.experimental.pallas import tpu as pltpu

F32 = jnp.float32
BF16 = jnp.bfloat16

EPS = 1e-6
N_HEADS = 8
HEAD_DIM = 64
ATTN_WIDTH = N_HEADS * HEAD_DIM
CONV_K = 31
N_GROUPS = 4
EXPERTS_PER_GROUP = 4
N_EXPERTS = N_GROUPS * EXPERTS_PER_GROUP
N_MOD = 6
LANES = 128
SUBLANES = 8
N_PAIRS = 6
N_CLASSES = N_GROUPS * N_PAIRS
CLASS_ROWS = 32
MOE_TILE = 512
AUG_PER_HEAD = 6
LOGF_LANE0 = AUG_PER_HEAD * N_HEADS
CONV_HALO = 32
NEG_BIG = -1e30
VMEM_LIMIT = 56 * 1024 * 1024


def _dot(a, b):
    return jnp.dot(a, b, preferred_element_type=F32)


def _dot_nt(a, b):
    return lax.dot_general(a, b, (((1,), (1,)), ((), ())), preferred_element_type=F32)


def _split2(x):
    hi = x.astype(BF16)
    lo = (x - hi.astype(F32)).astype(BF16)
    return hi, lo


def _split3(x):
    hi = x.astype(BF16)
    r1 = x - hi.astype(F32)
    mid = r1.astype(BF16)
    lo = (r1 - mid.astype(F32)).astype(BF16)
    return hi, mid, lo


def _dot_hp(a, b):
    ah, al = _split2(a)
    bh, bl = _split2(b)
    return _dot(ah, bh) + (_dot(ah, bl) + _dot(al, bh))


def _sigmoid(x):
    return 1.0 / (1.0 + jnp.exp(-x))


def _log_sigmoid(x):
    return jnp.minimum(x, 0.0) - jnp.log1p(jnp.exp(-jnp.abs(x)))


def _rms(x, g):
    return x * lax.rsqrt(jnp.mean(x * x, axis=-1, keepdims=True) + EPS) * g


def _cparams(sem):
    return pltpu.CompilerParams(dimension_semantics=sem, vmem_limit_bytes=VMEM_LIMIT)


def _ada_kernel(c_ref, w_ref, b_ref, o_ref):
    c = c_ref[...]
    o_ref[...] = _dot_hp(c * _sigmoid(c), w_ref[...]) + b_ref[...]


def _ada(c, w, b, bn=1024):
    r, d = c.shape
    n = w.shape[1]
    return pl.pallas_call(
        _ada_kernel,
        grid=(n // bn,),
        in_specs=[pl.BlockSpec((r, d), lambda j: (0, 0)),
                  pl.BlockSpec((d, bn), lambda j: (0, j)),
                  pl.BlockSpec((1, bn), lambda j: (0, j))],
        out_specs=pl.BlockSpec((r, bn), lambda j: (0, j)),
        out_shape=jax.ShapeDtypeStruct((r, n), F32),
        compiler_params=_cparams(("arbitrary",)),
        name="ada",
    )(c, w, b.reshape(1, n))


def _conv_tail(y, conv_b, ln_g, ln_b):
    yf = y + conv_b
    mu = jnp.mean(yf, axis=-1, keepdims=True)
    d = yf - mu
    yn = d * lax.rsqrt(jnp.mean(d * d, axis=-1, keepdims=True) + EPS) * ln_g + ln_b
    return yn * _sigmoid(yn)


def _inproj_kernel(*refs, tm, n_i, prompt, conv_rows):
    if prompt:
        (x_ref, mod_ref, g_ref, w_ref, bf_ref, cw_ref, cb_ref, lg_ref, lb_ref,
         q_ref, k_ref, v_ref, lf_ref, kb_ref, vb_ref, qa_ref, ka_ref, cv_ref, st_ref,
         ubuf, fcarry) = refs
    else:
        (x_ref, mod_ref, g_ref, w_ref, bf_ref,
         q_ref, k_ref, v_ref, lf_ref, u_ref) = refs
    i = pl.program_id(1)
    a = ATTN_WIDTH
    cwid = (w_ref.shape[1] - 3 * a - LANES) // 2
    o = 3 * a

    if prompt:
        @pl.when(i == 0)
        def _():
            ubuf[0, 0:CONV_HALO, :] = jnp.zeros((CONV_HALO, cwid), F32)
            fcarry[...] = jnp.zeros_like(fcarry)

    x = x_ref[0]
    m = mod_ref[0]
    hb = (_rms(x, g_ref[...]) * (1.0 + m[1]) + m[0]).astype(BF16)

    u = _dot(hb, w_ref[:, o:o + cwid]) * _sigmoid(_dot(hb, w_ref[:, o + cwid:o + 2 * cwid]))
    q_ref[0] = (_dot(hb, w_ref[:, 0:a]) * (HEAD_DIM ** -0.5)).astype(BF16)
    k = _dot(hb, w_ref[:, a:2 * a])
    k_ref[0] = k
    v = _dot(hb, w_ref[:, 2 * a:3 * a])
    v_ref[0] = v
    lf = _log_sigmoid(_dot(hb, w_ref[:, o + 2 * cwid:]) + bf_ref[...])
    lf_ref[0] = lf[:, LOGF_LANE0:LOGF_LANE0 + N_HEADS]
    if not prompt:
        u_ref[0] = u
        return

    kb_ref[0] = k.astype(BF16)
    vb_ref[0] = v.astype(BF16)

    cb = 256 if tm % 256 == 0 else tm
    row = lax.broadcasted_iota(jnp.int32, (cb, cb), 0)
    col = lax.broadcasted_iota(jnp.int32, (cb, cb), 1)
    tri = (row >= col).astype(BF16)
    parts = jnp.concatenate(_split3(lf), axis=1)
    carry = fcarry[...]
    lane = lax.broadcasted_iota(jnp.int32, (1, LANES), 1)
    r = lane % AUG_PER_HEAD
    valid = lane < LOGF_LANE0
    for c in range(tm // cb):
        p3 = _dot(tri, parts[c * cb:(c + 1) * cb, :])
        fc = (p3[:, 0:LANES] + p3[:, LANES:2 * LANES] + p3[:, 2 * LANES:]) + carry
        carry = fc[cb - 1:cb, :]
        fh = fc.astype(BF16).astype(F32)
        r1 = fc - fh
        fm = r1.astype(BF16).astype(F32)
        fl = (r1 - fm).astype(BF16).astype(F32)
        qa = jnp.where(r == 0, fh, jnp.where(r == 1, fm, jnp.where(r == 2, fl, 1.0)))
        ka = jnp.where(r < 3, 1.0, jnp.where(r == 3, -fh, jnp.where(r == 4, -fm, -fl)))
        qa_ref[0, c * cb:(c + 1) * cb, :] = jnp.where(valid, qa, 0.0).astype(BF16)
        ka_ref[0, c * cb:(c + 1) * cb, :] = jnp.where(valid, ka, 0.0).astype(BF16)
    fcarry[...] = carry

    nb = tm + CONV_HALO
    ubuf[0, CONV_HALO:nb, :] = u
    for r in range(1, SUBLANES):
        ubuf[r, SUBLANES:nb, :] = ubuf[0, SUBLANES - r:nb - r, :]
    groups = conv_rows // SUBLANES
    for c in range(tm // conv_rows):
        base = CONV_HALO + c * conv_rows
        acc = None
        for d in range(CONV_K):
            a8, r = (d // SUBLANES) * SUBLANES, d % SUBLANES
            rows = ubuf[r, base - a8:base - a8 + conv_rows, :].reshape(groups, SUBLANES, cwid)
            term = cw_ref[CONV_K - 1 - d][None] * rows
            acc = term if acc is None else acc + term
        cv_ref[0, c * conv_rows:(c + 1) * conv_rows, :] = _conv_tail(
            acc.reshape(conv_rows, cwid), cb_ref[...], lg_ref[...], lb_ref[...]).astype(BF16)

    @pl.when(i == n_i - 1)
    def _():
        st_ref[0] = ubuf[0, nb - (CONV_K - 1):nb, :]

    ubuf[0, 0:CONV_HALO, :] = ubuf[0, tm:nb, :]


def _inproj(x, mod, g, w_cat, bf_pad, conv=None, tm=512):
    b, s, d = x.shape
    tm = min(tm, s)
    n_i = s // tm
    a = ATTN_WIDTH
    cwid = (w_cat.shape[1] - 3 * a - LANES) // 2
    rmod = mod.shape[2]
    mod_spec = (pl.BlockSpec((1, N_MOD, 1, d), lambda bi, i: (bi, 0, 0, 0)) if rmod == 1 else
                pl.BlockSpec((1, N_MOD, tm, d), lambda bi, i: (bi, 0, i, 0)))
    tile = lambda w: pl.BlockSpec((1, tm, w), lambda bi, i: (bi, i, 0))
    full = lambda arr: pl.BlockSpec(arr.shape, lambda bi, i: (0,) * arr.ndim)
    prompt = conv is not None
    ins = [x, mod, g, w_cat, bf_pad]
    in_specs = [tile(d), mod_spec, full(g), full(w_cat), full(bf_pad)]
    outs = [jax.ShapeDtypeStruct((b, s, a), BF16), jax.ShapeDtypeStruct((b, s, a), F32),
            jax.ShapeDtypeStruct((b, s, a), F32), jax.ShapeDtypeStruct((b, s, N_HEADS), F32)]
    out_specs = [tile(a), tile(a), tile(a), tile(N_HEADS)]
    scratch = []
    if prompt:
        ins += list(conv)
        in_specs += [full(c) for c in conv]
        outs += [jax.ShapeDtypeStruct((b, s, a), BF16), jax.ShapeDtypeStruct((b, s, a), BF16),
                 jax.ShapeDtypeStruct((b, s, LANES), BF16), jax.ShapeDtypeStruct((b, s, LANES), BF16),
                 jax.ShapeDtypeStruct((b, s, cwid), BF16), jax.ShapeDtypeStruct((b, CONV_K - 1, cwid), F32)]
        out_specs += [tile(a), tile(a), tile(LANES), tile(LANES), tile(cwid),
                      pl.BlockSpec((1, CONV_K - 1, cwid), lambda bi, i: (bi, 0, 0))]
        scratch = [pltpu.VMEM((SUBLANES, tm + CONV_HALO, cwid), F32), pltpu.VMEM((1, LANES), F32)]
    else:
        outs += [jax.ShapeDtypeStruct((b, s, cwid), F32)]
        out_specs += [tile(cwid)]
    return pl.pallas_call(
        functools.partial(_inproj_kernel, tm=tm, n_i=n_i, prompt=prompt, conv_rows=min(32, tm)),
        grid=(b, n_i),
        in_specs=in_specs,
        out_specs=out_specs,
        out_shape=outs,
        scratch_shapes=scratch,
        compiler_params=_cparams(("arbitrary", "arbitrary")),
        name="inproj_prompt" if prompt else "inproj_sample",
    )(*ins)


def _attn_kernel(q_ref, qa_ref, k_ref, ka_ref, v_ref, o_ref, lhs_scr, *, tq):
    i = pl.program_id(1)
    lane = lax.broadcasted_iota(jnp.int32, (1, LANES), 1)
    lo = lane < HEAD_DIM
    qa = qa_ref[0]
    zero = jnp.zeros((), BF16)
    n_slabs = N_HEADS // 2

    for h in range(N_HEADS):
        q2 = q_ref[0, :, (h // 2) * LANES:(h // 2 + 1) * LANES]
        lhs_scr[h, :, 0:LANES] = jnp.where(lo if h % 2 == 0 else jnp.logical_not(lo), q2, zero)
        lhs_scr[h, :, LANES:] = jnp.where((lane >= AUG_PER_HEAD * h) & (lane < AUG_PER_HEAD * (h + 1)), qa, zero)

    def step(t, carry, masked, width=1):
        ms, ls, accs = carry
        nk = width * tq
        ks = pl.multiple_of(t * tq, tq)
        kaug = ka_ref[0, pl.ds(ks, nk), :]
        ones = jnp.ones((nk, LANES), BF16)
        if masked:
            causal = (lax.broadcasted_iota(jnp.int32, (tq, tq), 1) <=
                      lax.broadcasted_iota(jnp.int32, (tq, tq), 0))
        new_m, new_l, new_acc = [], [], []
        for j in range(n_slabs):
            sl = slice(j * LANES, (j + 1) * LANES)
            rhs = jnp.concatenate([k_ref[0, pl.ds(ks, nk), sl], kaug], axis=1)
            v2 = jnp.concatenate([v_ref[0, pl.ds(ks, nk), sl], ones], axis=1)
            alphas, pvs = [], []
            for h in (2 * j, 2 * j + 1):
                s = _dot_nt(lhs_scr[h], rhs)
                if masked:
                    s = jnp.where(causal, s, -jnp.inf)
                m_new = jnp.maximum(ms[h], jnp.max(s, axis=1, keepdims=True))
                alphas.append(jnp.exp(ms[h] - m_new))
                new_m.append(m_new)
                pvs.append(_dot(jnp.exp(s - m_new).astype(BF16), v2))
            alpha2 = jnp.where(lo, alphas[0], alphas[1])
            new_acc.append(accs[j] * alpha2 + jnp.where(lo, pvs[0][:, 0:LANES], pvs[1][:, 0:LANES]))
            new_l.append(ls[j] * alpha2 + jnp.where(lo, pvs[0][:, LANES:], pvs[1][:, LANES:]))
        return tuple(new_m), tuple(new_l), tuple(new_acc)

    init = (tuple(jnp.full((tq, 1), NEG_BIG, F32) for _ in range(N_HEADS)),
            tuple(jnp.zeros((tq, LANES), F32) for _ in range(n_slabs)),
            tuple(jnp.zeros((tq, LANES), F32) for _ in range(n_slabs)))
    carry = lax.fori_loop(0, i // 2, lambda n, c: step(2 * n, c, False, width=2), init)
    carry = lax.fori_loop(0, i % 2, lambda n, c: step(i - 1, c, False), carry)
    _, ls, accs = step(i, carry, True)
    for j in range(n_slabs):
        o_ref[0, :, j * LANES:(j + 1) * LANES] = (accs[j] / ls[j]).astype(BF16)


def _attn(q, qa, kb, ka, vb, tq=512):
    b, s, a = q.shape
    tq = min(tq, s)
    tile = lambda w: pl.BlockSpec((1, tq, w), lambda bi, i: (bi, i, 0))
    whole = lambda w: pl.BlockSpec((1, s, w), lambda bi, i: (bi, 0, 0))
    return pl.pallas_call(
        functools.partial(_attn_kernel, tq=tq),
        grid=(b, s // tq),
        in_specs=[tile(a), tile(LANES), whole(a), whole(LANES), whole(a)],
        out_specs=tile(a),
        out_shape=jax.ShapeDtypeStruct((b, s, a), BF16),
        scratch_shapes=[pltpu.VMEM((N_HEADS, tq, 2 * LANES), BF16)],
        compiler_params=_cparams(("arbitrary", "arbitrary")),
        name="attn_prompt",
    )(q, qa, kb, ka, vb)


def _sconv_kernel(u_ref, st_ref, cw_ref, cb_ref, lg_ref, lb_ref, cv_ref, nst_ref):
    u = u_ref[...]
    y = u * cw_ref[CONV_K - 1:CONV_K, :]
    for j in range(CONV_K - 1):
        y = y + st_ref[j] * cw_ref[j:j + 1, :]
    cv_ref[...] = _conv_tail(y, cb_ref[...], lg_ref[...], lb_ref[...]).astype(BF16)
    for j in range(CONV_K - 2):
        nst_ref[j] = st_ref[j + 1]
    nst_ref[CONV_K - 2] = u


def _sconv(u, state, cw, cb, lg, lb):
    bd, c = u.shape
    return pl.pallas_call(
        _sconv_kernel,
        out_shape=[jax.ShapeDtypeStruct((bd, c), BF16), jax.ShapeDtypeStruct(state.shape, F32)],
        compiler_params=pltpu.CompilerParams(vmem_limit_bytes=VMEM_LIMIT),
        name="conv_sample",
    )(u, state, cw, cb, lg, lb)


def _decode_kernel(pt_ref, q_ref, kn_ref, vn_ref, lfn_ref, *refs, pp, n_steps):
    kt_refs = refs[0:pp]
    vt_refs = refs[pp:2 * pp]
    lf_refs = refs[2 * pp:3 * pp]
    o_ref = refs[3 * pp]
    m_scr, l_scr, acc_scr, f_scr = refs[3 * pp + 1:]
    st = pl.program_id(1)
    a = ATTN_WIDTH
    page = kt_refs[0].shape[2]

    @pl.when(st == 0)
    def _():
        m_scr[...] = jnp.full_like(m_scr, NEG_BIG)
        l_scr[...] = jnp.zeros_like(l_scr)
        acc_scr[...] = jnp.zeros_like(acc_scr)
        f_scr[...] = jnp.zeros_like(f_scr)

    head_of_lane = lax.broadcasted_iota(jnp.int32, (N_HEADS, a), 1) // HEAD_DIM
    head_row = lax.broadcasted_iota(jnp.int32, (N_HEADS, a), 0)
    diag = head_of_lane == head_row
    qbd = jnp.where(diag, jnp.broadcast_to(q_ref[0].astype(F32), (N_HEADS, a)), 0.0)
    qbd_b = qbd.astype(BF16)

    rowi = lax.broadcasted_iota(jnp.int32, (page, page), 0)
    coli = lax.broadcasted_iota(jnp.int32, (page, page), 1)
    triu = (rowi <= coli).astype(BF16)
    fc = f_scr[...]
    s_parts = []
    for r in range(pp):
        lf = lf_refs[r][0]
        p3 = _dot(jnp.concatenate(_split3(lf), axis=0), triu)
        fpre = (p3[0:N_HEADS] + p3[N_HEADS:2 * N_HEADS] + p3[2 * N_HEADS:]) + fc
        fc = fc + jnp.sum(lf, axis=1, keepdims=True)
        s_parts.append(_dot(qbd_b, kt_refs[r][0].astype(BF16)) - fpre)
    f_scr[...] = fc
    s = jnp.concatenate(s_parts, axis=1)
    m_old = m_scr[...]
    m_new = jnp.maximum(m_old, jnp.max(s, axis=1, keepdims=True))
    alpha = jnp.exp(m_old - m_new)
    p = jnp.exp(s - m_new)
    l_new = alpha * l_scr[...] + jnp.sum(p, axis=1, keepdims=True)
    pb = p.astype(BF16)
    acc = acc_scr[...] * alpha
    for r in range(pp):
        acc = acc + _dot_nt(pb[:, r * page:(r + 1) * page], vt_refs[r][0].astype(BF16))
    m_scr[...] = m_new
    l_scr[...] = l_new
    acc_scr[...] = acc

    @pl.when(st == n_steps - 1)
    def _():
        eye8 = (lax.broadcasted_iota(jnp.int32, (N_HEADS, N_HEADS), 0) ==
                lax.broadcasted_iota(jnp.int32, (N_HEADS, N_HEADS), 1))
        f_new = fc + jnp.sum(jnp.where(eye8, jnp.broadcast_to(lfn_ref[0], (N_HEADS, N_HEADS)), 0.0),
                             axis=1, keepdims=True)
        s_self = jnp.sum(qbd * kn_ref[0], axis=1, keepdims=True) - f_new
        m_f = jnp.maximum(m_new, s_self)
        al = jnp.exp(m_new - m_f)
        p_self = jnp.exp(s_self - m_f)
        l_f = l_new * al + p_self
        acc_f = acc * al + p_self * vn_ref[0]
        o = jnp.where(diag, acc_f / l_f, 0.0)
        o_ref[0] = jnp.sum(o, axis=0, keepdims=True).astype(BF16)


def _decode(q, k_new, v_new, logf_new, cache_k, cache_v, cache_logf, page_table, pp=32):
    bd, n_pages = page_table.shape
    a = ATTN_WIDTH
    page = cache_k.shape[2]
    pp = min(pp, n_pages)
    n_steps = n_pages // pp
    row = lambda w: pl.BlockSpec((1, 1, w), lambda b, s, pt: (b, 0, 0))

    def paged(w, r):
        return pl.BlockSpec((1, w, page), lambda b, s, pt: (pt[b * n_pages + s * pp + r], 0, 0))

    in_specs = ([row(a), row(a), row(a), row(N_HEADS)] +
                [paged(a, r) for r in range(pp)] + [paged(a, r) for r in range(pp)] +
                [paged(N_HEADS, r) for r in range(pp)])
    return pl.pallas_call(
        functools.partial(_decode_kernel, pp=pp, n_steps=n_steps),
        grid_spec=pltpu.PrefetchScalarGridSpec(
            num_scalar_prefetch=1,
            grid=(bd, n_steps),
            in_specs=in_specs,
            out_specs=row(a),
            scratch_shapes=[pltpu.VMEM((N_HEADS, 1), F32), pltpu.VMEM((N_HEADS, 1), F32),
                            pltpu.VMEM((N_HEADS, a), F32), pltpu.VMEM((N_HEADS, 1), F32)]),
        out_shape=jax.ShapeDtypeStruct((bd, 1, a), BF16),
        compiler_params=_cparams(("arbitrary", "arbitrary")),
        name="decode_attn",
    )(page_table.reshape(-1), q, k_new, v_new, logf_new,
      *([cache_k] * pp), *([cache_v] * pp), *([cache_logf] * pp))


def _outproj_kernel(a_ref, cv_ref, x_ref, mod_ref, g_ref, wo_ref, wr_ref, br_ref, *refs, tm, sparse):
    if sparse:
        xp_ref, t_ref, ri_ref, cnt_ref, gt_scr, cnt_scr = refs
    else:
        xp_ref, t_ref, gate_ref, gt_scr = refs
    aw = a_ref.shape[2]
    d = x_ref.shape[2]
    m = mod_ref[0]
    y = _dot(a_ref[0], wo_ref[0:aw, :]) + _dot(cv_ref[0], wo_ref[aw:, :])
    xp = x_ref[0] + m[2] * y
    xp_ref[0] = xp
    t = _rms(xp, g_ref[...]) * (1.0 + m[4]) + m[3]
    if sparse:
        t_ref[0, :, 0:d] = t
    else:
        t_ref[0] = t.astype(BF16)

    th, tl = _split2(t)
    r2 = _dot(th, wr_ref[...])
    lg = (r2[:, 0:LANES] + r2[:, LANES:]) + _dot(tl, wr_ref[:, 0:LANES]) + br_ref[...]
    lt = lg.T
    g = [lt[r:r + 1, :] for r in range(N_GROUPS)]
    gmax = jnp.maximum(jnp.maximum(g[0], g[1]), jnp.maximum(g[2], g[3]))
    gidx = jnp.where(g[0] >= gmax, 0, jnp.where(g[1] >= gmax, 1, jnp.where(g[2] >= gmax, 2, 3)))
    gden = (jnp.exp(g[0] - gmax) + jnp.exp(g[1] - gmax)) + (jnp.exp(g[2] - gmax) + jnp.exp(g[3] - gmax))
    gw = 1.0 / gden
    e = []
    for r in range(EXPERTS_PER_GROUP):
        er = jnp.zeros_like(gmax)
        for gi in range(N_GROUPS):
            c = N_GROUPS + gi * EXPERTS_PER_GROUP + r
            er = jnp.where(gidx == gi, lt[c:c + 1, :], er)
        e.append(er)
    v1 = jnp.maximum(jnp.maximum(e[0], e[1]), jnp.maximum(e[2], e[3]))
    i1 = jnp.where(e[0] >= v1, 0, jnp.where(e[1] >= v1, 1, jnp.where(e[2] >= v1, 2, 3)))
    rest = [jnp.where(i1 == r, -jnp.inf, e[r]) for r in range(EXPERTS_PER_GROUP)]
    v2 = jnp.maximum(jnp.maximum(rest[0], rest[1]), jnp.maximum(rest[2], rest[3]))
    i2 = jnp.where(rest[0] >= v2, 0, jnp.where(rest[1] >= v2, 1, jnp.where(rest[2] >= v2, 2, 3)))
    ex = jnp.exp(v2 - v1)
    w1 = gw / (1.0 + ex)
    w2 = gw * ex / (1.0 + ex)
    gt_scr[...] = jnp.zeros_like(gt_scr)
    if not sparse:
        id1 = gidx * EXPERTS_PER_GROUP + i1
        id2 = gidx * EXPERTS_PER_GROUP + i2
        for ei in range(N_EXPERTS):
            gt_scr[ei:ei + 1, :] = jnp.where(id1 == ei, w1, 0.0) + jnp.where(id2 == ei, w2, 0.0)
        gate_ref[0] = gt_scr[...].T[:, 0:N_EXPERTS]
        return

    first = i1 < i2
    lo = jnp.minimum(i1, i2)
    hi = jnp.maximum(i1, i2)
    pair = jnp.where(lo == 0, hi - 1, jnp.where(lo == 1, hi + 1, N_PAIRS - 1))
    cls = gidx * N_PAIRS + pair
    gt_scr[0:1, :] = jnp.where(first, w1, w2)
    gt_scr[1:2, :] = jnp.where(first, w2, w1)
    t_ref[0, :, d:] = gt_scr[...].T

    @pl.when((pl.program_id(0) == 0) & (pl.program_id(1) == 0))
    def _():
        cnt_scr[...] = jnp.zeros_like(cnt_scr)

    oh = (lax.broadcasted_iota(jnp.int32, (CLASS_ROWS, tm), 0) == cls).astype(F32)
    before = (lax.broadcasted_iota(jnp.int32, (tm, tm), 0) <
              lax.broadcasted_iota(jnp.int32, (tm, tm), 1)).astype(BF16)
    carry = cnt_scr[...]
    pref = _dot(oh.astype(BF16), before) + carry[:, 0:1]
    rank = jnp.sum(oh * pref, axis=0, keepdims=True)
    ri_ref[0] = jnp.zeros(ri_ref.shape[1:], jnp.int32)
    ri_ref[0, 0:1, :] = cls
    ri_ref[0, 1:2, :] = rank.astype(jnp.int32)
    total = carry + jnp.sum(oh, axis=1, keepdims=True)
    cnt_scr[...] = total
    cnt_ref[...] = total.astype(jnp.int32)


def _outproj(a, cv, x, mod, g, wo, wr_pad, br_pad, sparse, tm=512):
    b, s, d = x.shape
    tm = min(tm, s)
    aw, cw = a.shape[2], cv.shape[2]
    rmod = mod.shape[2]
    mod_spec = (pl.BlockSpec((1, N_MOD, 1, d), lambda bi, i: (bi, 0, 0, 0)) if rmod == 1 else
                pl.BlockSpec((1, N_MOD, tm, d), lambda bi, i: (bi, 0, i, 0)))
    tile = lambda w: pl.BlockSpec((1, tm, w), lambda bi, i: (bi, i, 0))
    full = lambda arr: pl.BlockSpec(arr.shape, lambda bi, i: (0,) * arr.ndim)
    if sparse:
        out_specs = [tile(d), tile(d + LANES), pl.BlockSpec((1, SUBLANES, tm), lambda bi, i: (bi, 0, i)),
                     pl.BlockSpec((CLASS_ROWS, LANES), lambda bi, i: (0, 0))]
        out_shape = [jax.ShapeDtypeStruct((b, s, d), F32), jax.ShapeDtypeStruct((b, s, d + LANES), F32),
                     jax.ShapeDtypeStruct((b, SUBLANES, s), jnp.int32),
                     jax.ShapeDtypeStruct((CLASS_ROWS, LANES), jnp.int32)]
        scratch = [pltpu.VMEM((LANES, tm), F32), pltpu.VMEM((CLASS_ROWS, LANES), F32)]
    else:
        out_specs = [tile(d), tile(d), tile(N_EXPERTS)]
        out_shape = [jax.ShapeDtypeStruct((b, s, d), F32), jax.ShapeDtypeStruct((b, s, d), BF16),
                     jax.ShapeDtypeStruct((b, s, N_EXPERTS), F32)]
        scratch = [pltpu.VMEM((LANES, tm), F32)]
    return pl.pallas_call(
        functools.partial(_outproj_kernel, tm=tm, sparse=sparse),
        grid=(b, s // tm),
        in_specs=[tile(aw), tile(cw), tile(d), mod_spec, full(g), full(wo), full(wr_pad), full(br_pad)],
        out_specs=out_specs,
        out_shape=out_shape,
        scratch_shapes=scratch,
        compiler_params=_cparams(("arbitrary", "arbitrary")),
        name="outproj_router_sparse" if sparse else "outproj_router_dense",
    )(a, cv, x, mod, g, wo, wr_pad, br_pad)


def _dispatch_kernel(zf_ref, dest_ref, src_ref, out_ref, zbuf, sem, *, td, tme, n_tiles):
    i = pl.program_id(0)

    def zero_copy(j):
        return pltpu.make_async_copy(zbuf, out_ref.at[pl.ds(pl.multiple_of(j * tme, tme), tme)], sem.at[0])

    @pl.when(i == 0)
    def _():
        zbuf[...] = jnp.zeros_like(zbuf)

        def start(j, c):
            @pl.when(zf_ref[j] == 1)
            def _():
                zero_copy(j).start()
            return c

        def wait(j, c):
            @pl.when(zf_ref[j] == 1)
            def _():
                zero_copy(j).wait()
            return c

        lax.fori_loop(0, n_tiles, start, 0)
        lax.fori_loop(0, n_tiles, wait, 0)

    def row_copy(c, k):
        return pltpu.make_async_copy(src_ref.at[c, pl.ds(k, 1)],
                                     out_ref.at[pl.ds(dest_ref[0, 0, c * SUBLANES + k], 1)], sem.at[1])

    def start_rows(c, carry):
        for k in range(SUBLANES):
            row_copy(c, k).start(priority=k % 2)
        return carry

    def wait_rows(c, carry):
        for k in range(SUBLANES):
            row_copy(c, k).wait()
        return carry

    lax.fori_loop(0, td // SUBLANES, start_rows, 0)
    lax.fori_loop(0, td // SUBLANES, wait_rows, 0)


def _dispatch(rows, dest, zero_flag, n_tiles, tme, td=2048):
    n, w = rows.shape
    td = min(td, n)
    return pl.pallas_call(
        functools.partial(_dispatch_kernel, td=td, tme=tme, n_tiles=n_tiles),
        grid_spec=pltpu.PrefetchScalarGridSpec(
            num_scalar_prefetch=1,
            grid=(n // td,),
            in_specs=[pl.BlockSpec((1, 1, td), lambda i, zf: (i, 0, 0), memory_space=pltpu.SMEM),
                      pl.BlockSpec((td // SUBLANES, SUBLANES, w), lambda i, zf: (i, 0, 0))],
            out_specs=pl.BlockSpec(memory_space=pl.ANY),
            scratch_shapes=[pltpu.VMEM((tme, w), F32), pltpu.SemaphoreType.DMA((2,))]),
        out_shape=jax.ShapeDtypeStruct((n_tiles * tme, w), F32),
        compiler_params=_cparams(("arbitrary",)),
        name="moe_dispatch",
    )(zero_flag, dest.reshape(n // td, 1, td), rows.reshape(n // SUBLANES, SUBLANES, w))


def _moe_sorted_kernel(blk_ref, ea_ref, eb_ref, valid_ref, ts_ref, w1a, w3a, w2a, w1b, w3b, w2b, o_ref):
    j = pl.program_id(0)
    d = o_ref.shape[1]

    @pl.when(valid_ref[j] == 1)
    def _():
        t = ts_ref[:, 0:d].astype(BF16)

        def expert(w1, w3, w2):
            h1 = _dot(t, w1[0])
            he = (h1 * _sigmoid(h1)) * _dot(t, w3[0])
            return _dot(he.astype(BF16), w2[0])

        o_ref[...] = ts_ref[:, d:d + 1] * expert(w1a, w3a, w2a) + ts_ref[:, d + 1:d + 2] * expert(w1b, w3b, w2b)

    @pl.when(valid_ref[j] == 0)
    def _():
        o_ref[...] = jnp.zeros_like(o_ref)


def _moe_sorted(rows_sorted, blk, ea, eb, valid, w1, w3, w2, tme):
    p, w = rows_sorted.shape
    ne, d, de = w1.shape
    n_tiles = p // tme
    up = lambda sel: pl.BlockSpec((1, d, de), lambda j, blk, ea, eb, valid: ((ea, eb)[sel][j], 0, 0))
    down = lambda sel: pl.BlockSpec((1, de, d), lambda j, blk, ea, eb, valid: ((ea, eb)[sel][j], 0, 0))
    return pl.pallas_call(
        _moe_sorted_kernel,
        grid_spec=pltpu.PrefetchScalarGridSpec(
            num_scalar_prefetch=4,
            grid=(n_tiles,),
            in_specs=[pl.BlockSpec((tme, w), lambda j, blk, ea, eb, valid: (blk[j], 0)),
                      up(0), up(0), down(0), up(1), up(1), down(1)],
            out_specs=pl.BlockSpec((tme, d), lambda j, blk, ea, eb, valid: (j, 0))),
        out_shape=jax.ShapeDtypeStruct((p, d), F32),
        compiler_params=_cparams(("arbitrary",)),
        name="moe_sorted",
    )(blk, ea, eb, valid, rows_sorted, w1, w3, w2, w1, w3, w2)


def _final_kernel(dcur_ref, dnxt_ref, xp_ref, mod_ref, modf_ref, gf_ref, ys_ref, y_ref, gbuf, sem, *, tm):
    step = pl.program_id(0) * pl.num_programs(1) + pl.program_id(1)
    n_steps = pl.num_programs(0) * pl.num_programs(1)
    slot = step % 2

    def row_copy(dref, c, k, s):
        return pltpu.make_async_copy(ys_ref.at[pl.ds(dref[0, 0, c * SUBLANES + k], 1)],
                                     gbuf.at[s, c, pl.ds(k, 1)], sem.at[s])

    def issue(dref, s):
        def body(c, carry):
            for k in range(SUBLANES):
                row_copy(dref, c, k, s).start(priority=k % 2)
            return carry
        lax.fori_loop(0, tm // SUBLANES, body, 0)

    @pl.when(step == 0)
    def _():
        issue(dcur_ref, 0)

    @pl.when(step + 1 < n_steps)
    def _():
        issue(dnxt_ref, 1 - slot)

    def wait_rows(c, carry):
        for k in range(SUBLANES):
            row_copy(dcur_ref, c, k, slot).wait()
        return carry

    lax.fori_loop(0, tm // SUBLANES, wait_rows, 0)
    xp2 = xp_ref[0] + mod_ref[0][5] * gbuf[slot].reshape(tm, xp_ref.shape[2])
    mf = modf_ref[0]
    y_ref[0] = _rms(xp2, gf_ref[...]) * (1.0 + mf[1]) + mf[0]


def _final(ys, dest, xp, mod, modf, gf, tm=512):
    b, s, d = xp.shape
    tm = min(tm, s)
    n_i = s // tm
    n_blk = b * n_i
    dest3 = dest.reshape(n_blk, 1, tm)
    tile = pl.BlockSpec((1, tm, d), lambda bi, i: (bi, i, 0))
    return pl.pallas_call(
        functools.partial(_final_kernel, tm=tm),
        grid=(b, n_i),
        in_specs=[pl.BlockSpec((1, 1, tm), lambda bi, i: (bi * n_i + i, 0, 0), memory_space=pltpu.SMEM),
                  pl.BlockSpec((1, 1, tm), lambda bi, i: (jnp.minimum(bi * n_i + i + 1, n_blk - 1), 0, 0),
                               memory_space=pltpu.SMEM),
                  tile,
                  pl.BlockSpec((1, N_MOD, 1, d), lambda bi, i: (bi, 0, 0, 0)),
                  pl.BlockSpec((1, 2, 1, d), lambda bi, i: (bi, 0, 0, 0)),
                  pl.BlockSpec(gf.shape, lambda bi, i: (0, 0)),
                  pl.BlockSpec(memory_space=pl.ANY)],
        out_specs=tile,
        out_shape=jax.ShapeDtypeStruct((b, s, d), F32),
        scratch_shapes=[pltpu.VMEM((2, tm // SUBLANES, SUBLANES, d), F32), pltpu.SemaphoreType.DMA((2,))],
        compiler_params=_cparams(("arbitrary", "arbitrary")),
        name="moe_combine_final",
    )(dest3, dest3, xp, mod, modf, gf, ys)


def _route_tables(cnt, cls, rank, n_tiles, tme):
    ntile_c = (cnt + tme - 1) // tme
    tile_end = jnp.cumsum(ntile_c)
    offs = (tile_end - ntile_c) * tme
    dest = offs[cls] + rank
    n_valid = tile_end[-1]
    j = jnp.arange(n_tiles, dtype=jnp.int32)
    valid = j < n_valid
    blk = jnp.where(valid, j, n_valid - 1)
    tcls = jnp.sum((tile_end[None, :] <= blk[:, None]).astype(jnp.int32), axis=1)
    tcls = jnp.minimum(tcls, N_CLASSES - 1)
    pair_lo = jnp.array([0, 0, 0, 1, 1, 2], jnp.int32)
    pair_hi = jnp.array([1, 2, 3, 2, 3, 3], jnp.int32)
    grp, pair = tcls // N_PAIRS, tcls % N_PAIRS
    ea = grp * EXPERTS_PER_GROUP + pair_lo[pair]
    eb = grp * EXPERTS_PER_GROUP + pair_hi[pair]
    zero_flag = jnp.logical_or(jnp.logical_not(valid), j == tile_end[tcls] - 1)
    return (dest.astype(jnp.int32), blk.astype(jnp.int32), ea, eb, valid.astype(jnp.int32),
            zero_flag.astype(jnp.int32))


def _moe_kernel(t_ref, gate_ref, xp_ref, mod_ref, modf_ref, gf_ref, w1_ref, w3_ref, w2_ref, y_ref, acc):
    e = pl.program_id(2)

    @pl.when(e == 0)
    def _():
        acc[...] = jnp.zeros_like(acc)

    t = t_ref[0]
    h1 = _dot(t, w1_ref[0])
    he = (h1 * _sigmoid(h1)) * _dot(t, w3_ref[0])
    gate = gate_ref[0]
    lane = lax.broadcasted_iota(jnp.int32, gate.shape, 1)
    gcol = jnp.sum(jnp.where(lane == e, gate, 0.0), axis=1, keepdims=True)
    acc[...] += gcol * _dot(he.astype(BF16), w2_ref[0])

    @pl.when(e == pl.num_programs(2) - 1)
    def _():
        xp2 = xp_ref[0] + mod_ref[0][5] * acc[...]
        mf = modf_ref[0]
        y_ref[0] = _rms(xp2, gf_ref[...]) * (1.0 + mf[1]) + mf[0]


def _moe(t, gate, xp, mod, modf, gf, w1, w3, w2, tm=512):
    b, s, d = xp.shape
    tm = min(tm, s)
    ne, _, de = w1.shape
    rmod = mod.shape[2]

    def mod_spec(n):
        return (pl.BlockSpec((1, n, 1, d), lambda bi, i, e: (bi, 0, 0, 0)) if rmod == 1 else
                pl.BlockSpec((1, n, tm, d), lambda bi, i, e: (bi, 0, i, 0)))

    tile = lambda w: pl.BlockSpec((1, tm, w), lambda bi, i, e: (bi, i, 0))
    return pl.pallas_call(
        _moe_kernel,
        grid=(b, s // tm, ne),
        in_specs=[tile(d), tile(ne), tile(d), mod_spec(N_MOD), mod_spec(2),
                  pl.BlockSpec(gf.shape, lambda bi, i, e: (0, 0)),
                  pl.BlockSpec((1, d, de), lambda bi, i, e: (e, 0, 0)),
                  pl.BlockSpec((1, d, de), lambda bi, i, e: (e, 0, 0)),
                  pl.BlockSpec((1, de, d), lambda bi, i, e: (e, 0, 0))],
        out_specs=tile(d),
        out_shape=jax.ShapeDtypeStruct((b, s, d), F32),
        scratch_shapes=[pltpu.VMEM((tm, d), F32)],
        compiler_params=_cparams(("arbitrary", "arbitrary", "arbitrary")),
        name="moe_final",
    )(t, gate, xp, mod, modf, gf, w1, w3, w2)


def _prep_w_in(w_in, b_f):
    a = ATTN_WIDTH
    d = w_in.shape[0]
    cwid = (w_in.shape[1] - 3 * a - N_HEADS) // 2
    wf = w_in[:, 3 * a:3 * a + N_HEADS]
    lane_head = jnp.concatenate([jnp.repeat(jnp.arange(N_HEADS), AUG_PER_HEAD), jnp.arange(N_HEADS)])
    npad = LANES - lane_head.shape[0]
    wf_pad = jnp.concatenate([wf[:, lane_head], jnp.zeros((d, npad), F32)], axis=1)
    bf_pad = jnp.concatenate([b_f[lane_head], jnp.zeros((npad,), F32)]).reshape(1, LANES)
    w_cat = jnp.concatenate([w_in[:, 0:3 * a], w_in[:, 3 * a + N_HEADS:], wf_pad], axis=1).astype(BF16)
    del cwid
    return w_cat, bf_pad


def kernel(x_prompt, x_sample, cache_k, cache_v, cache_logf, state_conv, page_table, c_prompt, c_sample, w_ada, b_ada, g_norm1, g_norm2, w_in, b_f, conv_w, conv_b, conv_ln_g, conv_ln_b, w_out, w_rg, b_rg, w_re, b_re, w1, w3, w2, w_ada_f, b_ada_f, g_final):
    depth = w_ada.shape[0]
    assert depth == 1
    b, s, d = x_prompt.shape
    bd, ds, _ = x_sample.shape
    assert ds == 1
    a = ATTN_WIDTH
    n_phys, page = cache_k.shape[1], cache_k.shape[2]

    c_all = jnp.concatenate([c_prompt, c_sample], axis=0)
    mod = _ada(c_all, w_ada[0], b_ada[0]).reshape(b + bd, N_MOD, d)
    modf = _ada(c_all, w_ada_f, b_ada_f).reshape(b + bd, 2, d)
    mp = mod[:b].reshape(b, N_MOD, 1, d)
    ms = mod[b:].transpose(1, 0, 2).reshape(1, N_MOD, bd, d)
    mfp = modf[:b].reshape(b, 2, 1, d)
    mfs = modf[b:].transpose(1, 0, 2).reshape(1, 2, bd, d)

    w_cat, bf_pad = _prep_w_in(w_in[0], b_f[0])
    row = lambda v: v.reshape(1, -1)
    g1, g2, gf = row(g_norm1[0]), row(g_norm2[0]), row(g_final)
    cw, cb, lg, lb = conv_w[0], row(conv_b[0]), row(conv_ln_g[0]), row(conv_ln_b[0])
    wo = w_out[0].astype(BF16)
    nr = N_GROUPS + N_EXPERTS
    wr_f32 = jnp.concatenate([w_rg[0], w_re[0], jnp.zeros((d, LANES - nr), F32)], axis=1)
    wr_pad = jnp.concatenate(_split2(wr_f32), axis=1)
    br_pad = jnp.concatenate([b_rg[0], b_re[0], jnp.zeros((LANES - nr,), F32)]).reshape(1, LANES)
    w1b, w3b, w2b = w1[0].astype(BF16), w3[0].astype(BF16), w2[0].astype(BF16)

    cw_rep = jnp.broadcast_to(cw[:, None, :], (CONV_K, SUBLANES, cw.shape[1]))
    q, k_p, v_p, lf_p, kb, vb, qa, ka, cv_p, st_p = _inproj(x_prompt, mp, g1, w_cat, bf_pad,
                                                           conv=(cw_rep, cb, lg, lb))
    a_p = _attn(q, qa, kb, ka, vb)
    xp1, rows, ri, cnt = _outproj(a_p, cv_p, x_prompt, mp, g2, wo, wr_pad, br_pad, sparse=True)
    n = b * s
    tme = min(MOE_TILE, n)
    n_tiles = n // tme + N_CLASSES
    dest, blk, ea, eb, valid, zero_flag = _route_tables(
        cnt[:N_CLASSES, 0], ri[:, 0, :].reshape(n), ri[:, 1, :].reshape(n), n_tiles, tme)
    rows_sorted = _dispatch(rows.reshape(n, d + LANES), dest, zero_flag, n_tiles, tme)
    ys = _moe_sorted(rows_sorted, blk, ea, eb, valid, w1b, w3b, w2b, tme)
    y_prompt = _final(ys, dest, xp1, mp, mfp, gf)

    xs = x_sample.reshape(1, bd, d)
    q_s, k_s, v_s, lf_s, u_s = _inproj(xs, ms, g1, w_cat, bf_pad)
    cv_s, st_s = _sconv(u_s[0], state_conv[0].transpose(1, 0, 2), cw, cb, lg, lb)
    a_s = _decode(q_s.reshape(bd, 1, a), k_s.reshape(bd, 1, a), v_s.reshape(bd, 1, a),
                  lf_s.reshape(bd, 1, N_HEADS),
                  cache_k[0].transpose(0, 2, 3, 1).reshape(n_phys, a, page),
                  cache_v[0].transpose(0, 2, 3, 1).reshape(n_phys, a, page),
                  cache_logf[0].transpose(0, 2, 1), page_table)
    xs1, t_s, gate_s = _outproj(a_s.reshape(1, bd, a), cv_s.reshape(1, bd, -1), xs, ms, g2, wo, wr_pad, br_pad,
                                sparse=False)
    y_sample = _moe(t_s, gate_s, xs1, ms, mfs, gf, w1b, w3b, w2b).reshape(bd, 1, d)

    hs = (N_HEADS, HEAD_DIM)
    return (y_prompt, y_sample,
            k_p.reshape(1, b, s, *hs), v_p.reshape(1, b, s, *hs), lf_p.reshape(1, b, s, N_HEADS), st_p[None],
            k_s.reshape(1, bd, 1, *hs), v_s.reshape(1, bd, 1, *hs), lf_s.reshape(1, bd, 1, N_HEADS),
            st_s.transpose(1, 0, 2)[None])
```
